```python
import math
import jax
import jax.numpy as jnp
from jax import lax
import numpy as np

D_MODEL = 1024
BATCH = 16
SEQ = 4096
DEPTH = 1
DEC_BATCH = 2
DEC_SEQ = 8192
PAST_LEN = 128

HEAD_DIM = 64
D_LRU = D_MODEL // 2
LRU_BLOCKS = D_LRU // HEAD_DIM
LRU_BLK = D_LRU // LRU_BLOCKS
LRU_C = 8.0
CONV_W = 4
CONV_LEFT = 2
ATT_HEADS = (D_MODEL // 2) // HEAD_DIM
KV_HEADS = 2
Q_GROUP = ATT_HEADS // KV_HEADS
D_ATT = ATT_HEADS * HEAD_DIM
KV_DIM = KV_HEADS * HEAD_DIM
D_IN = 2 * D_LRU + D_ATT + 2 * KV_DIM
SPLITS = (D_LRU, 2 * D_LRU, 2 * D_LRU + D_ATT, 2 * D_LRU + D_ATT + KV_DIM)
GRID_W = 64
ROPE_THETA = 10000.0
ROPE_AXIS_FREQS = HEAD_DIM // 4
Q_BLOCK = 128
N_MEM = 256
X_HEADS = 4
X_HEAD_DIM = D_MODEL // X_HEADS
N_EXPERTS = 64
TOP_K = 8
N_GROUPS = 8
TOPK_GROUPS = 4
D_EXPERT = D_MODEL // 4
ROUTED_SCALE = 2.5
MOE_BLOCK = 128
LN_EPS = 1e-5
RMS_EPS = 1e-6

kernel_name = 'hybrid_rglru_gqa_moe_encoder'


def layer_norm(x, g, b):
    xf = x.astype(jnp.float32)
    mu = jnp.mean(xf, axis=-1, keepdims=True)
    xc = xf - mu
    var = jnp.mean(xc * xc, axis=-1, keepdims=True)
    return (xc * lax.rsqrt(var + LN_EPS) * g.astype(jnp.float32) + b.astype(jnp.float32)).astype(x.dtype)


def rms_norm(x, g):
    xf = x.astype(jnp.float32)
    y = xf * lax.rsqrt(jnp.mean(xf * xf, axis=-1, keepdims=True) + RMS_EPS)
    return (y * g.astype(jnp.float32)).astype(x.dtype)


def axial_rope(seq_len):
    rows = seq_len // GRID_W
    row_id, col_id = jnp.meshgrid(jnp.arange(rows), jnp.arange(GRID_W), indexing='ij')
    row_id = row_id.reshape(-1).astype(jnp.float32)
    col_id = col_id.reshape(-1).astype(jnp.float32)
    inv_freq = ROPE_THETA ** (-jnp.arange(ROPE_AXIS_FREQS, dtype=jnp.float32) / ROPE_AXIS_FREQS)
    ang = jnp.concatenate([row_id[:, None] * inv_freq, col_id[:, None] * inv_freq], axis=-1)
    return jnp.cos(ang), jnp.sin(ang)


def apply_rope(x, cos, sin):
    xf = x.astype(jnp.float32)
    half = HEAD_DIM // 2
    x1, x2 = xf[..., :half], xf[..., half:]
    c = cos[None, :, None, :]
    s = sin[None, :, None, :]
    return jnp.concatenate([x1 * c - x2 * s, x2 * c + x1 * s], axis=-1).astype(x.dtype)


def centred_dwconv(x, w, b):
    seq = x.shape[1]
    xp = jnp.pad(x, ((0, 0), (CONV_LEFT, CONV_W - 1 - CONV_LEFT), (0, 0)))
    out = xp[:, 0:seq] * w[0]
    for k in range(1, CONV_W):
        out = out + xp[:, k:k + seq] * w[k]
    return out + b


def lru_combine(left, right):
    a1, b1 = left
    a2, b2 = right
    return a1 * a2, a2 * b1 + b2


def rg_lru(xc, w_a, b_a, w_x, b_x, lam, reverse):
    bsz, seq, width = xc.shape
    xh = xc.reshape(bsz, seq, LRU_BLOCKS, LRU_BLK)
    r = jax.nn.sigmoid(jnp.einsum('bshi,hij->bshj', xh, w_a).reshape(bsz, seq, width) + b_a)
    i = jax.nn.sigmoid(jnp.einsum('bshi,hij->bshj', xh, w_x).reshape(bsz, seq, width) + b_x)
    log_a = -LRU_C * r.astype(jnp.float32) * jax.nn.softplus(-lam.astype(jnp.float32))
    a = jnp.exp(log_a)
    b = jnp.sqrt(-jnp.expm1(2.0 * log_a)) * (i * xc).astype(jnp.float32)
    _, h = lax.associative_scan(lru_combine, (a, b), axis=1, reverse=reverse)
    return h.astype(xc.dtype)


def block_attention(q, k, v):
    bsz, seq = q.shape[0], q.shape[1]
    n_blk = seq // Q_BLOCK
    qb = jnp.moveaxis(q.reshape(bsz, n_blk, Q_BLOCK, KV_HEADS, Q_GROUP, HEAD_DIM), 1, 0)
    scale = HEAD_DIM ** -0.5

    def one_block(q_blk):
        s = jnp.einsum('bqkgd,bskd->bkgqs', q_blk, k, preferred_element_type=jnp.float32) * scale
        p = jax.nn.softmax(s, axis=-1).astype(v.dtype)
        return jnp.einsum('bkgqs,bskd->bqkgd', p, v)

    o = lax.map(one_block, qb)
    return jnp.moveaxis(o, 0, 1).reshape(bsz, seq, D_ATT)


def parallel_mixer(x, lp):
    bsz, seq, _ = x.shape
    z = x @ lp['w_in']
    g_br, x_br, q, k, v = jnp.split(z, SPLITS, axis=-1)
    xc = centred_dwconv(x_br, lp['conv_w'], lp['conv_b'])
    h_fwd = rg_lru(xc, lp['lru_wa'][0], lp['lru_ba'][0], lp['lru_wx'][0], lp['lru_bx'][0], lp['lru_lambda'][0], False)
    h_bwd = rg_lru(xc, lp['lru_wa'][1], lp['lru_ba'][1], lp['lru_wx'][1], lp['lru_bx'][1], lp['lru_lambda'][1], True)
    y_lru = (h_fwd + h_bwd) * jax.nn.gelu(g_br)
    q = rms_norm(q.reshape(bsz, seq, ATT_HEADS, HEAD_DIM), lp['q_norm_g'])
    k = rms_norm(k.reshape(bsz, seq, KV_HEADS, HEAD_DIM), lp['k_norm_g'])
    v = v.reshape(bsz, seq, KV_HEADS, HEAD_DIM)
    cos, sin = axial_rope(seq)
    q = apply_rope(q, cos, sin).reshape(bsz, seq, KV_HEADS, Q_GROUP, HEAD_DIM)
    k = apply_rope(k, cos, sin)
    y_att = block_attention(q, k, v)
    y = jnp.concatenate([rms_norm(y_lru, lp['gn_lru_g']), rms_norm(y_att, lp['gn_att_g'])], axis=-1)
    return y @ lp['w_out']


def cross_attention(x, mem, lp):
    bsz, seq, _ = x.shape
    n_mem = mem.shape[1]
    q = (x @ lp['xa_wq']).reshape(bsz, seq, X_HEADS, X_HEAD_DIM)
    k, v = jnp.split(mem @ lp['xa_wkv'], 2, axis=-1)
    k = k.reshape(bsz, n_mem, X_HEADS, X_HEAD_DIM)
    v = v.reshape(bsz, n_mem, X_HEADS, X_HEAD_DIM)
    s = jnp.einsum('bshd,bmhd->bhsm', q, k, preferred_element_type=jnp.float32) * (X_HEAD_DIM ** -0.5)
    p = jax.nn.softmax(s, axis=-1).astype(v.dtype)
    o = jnp.einsum('bhsm,bmhd->bshd', p, v).reshape(bsz, seq, D_MODEL)
    return o @ lp['xa_wo']


def swiglu(x, wg, wu, wd):
    return (jax.nn.silu(x @ wg) * (x @ wu)) @ wd


def routed_experts(xt, router_w, router_b, w_gate, w_up, w_down):
    n, d = xt.shape
    scores = jax.nn.sigmoid(jnp.dot(xt, router_w, preferred_element_type=jnp.float32))
    biased = scores + router_b.astype(jnp.float32)
    grp = biased.reshape(n, N_GROUPS, N_EXPERTS // N_GROUPS)
    grp_score = lax.top_k(grp, 2)[0].sum(-1)
    _, gsel = lax.top_k(grp_score, TOPK_GROUPS)
    gmask = jax.nn.one_hot(gsel, N_GROUPS, dtype=jnp.float32).sum(1) > 0
    emask = jnp.repeat(gmask, N_EXPERTS // N_GROUPS, axis=1)
    _, eidx = lax.top_k(jnp.where(emask, biased, -jnp.inf), TOP_K)
    gate = jnp.take_along_axis(scores, eidx, axis=1)
    gate = gate / jnp.sum(gate, axis=-1, keepdims=True) * ROUTED_SCALE
    n_assign = n * TOP_K
    flat_e = eidx.reshape(-1).astype(jnp.int32)
    flat_t = jnp.repeat(jnp.arange(n, dtype=jnp.int32), TOP_K)
    flat_g = gate.reshape(-1)
    order = jnp.argsort(flat_e)
    e_sorted = flat_e[order]
    tok_sorted = flat_t[order]
    gate_sorted = flat_g[order]
    counts = jnp.bincount(flat_e, length=N_EXPERTS)
    pad_counts = ((counts + MOE_BLOCK - 1) // MOE_BLOCK) * MOE_BLOCK
    start = jnp.cumsum(counts) - counts
    pad_start = jnp.cumsum(pad_counts) - pad_counts
    dest = pad_start[e_sorted] + jnp.arange(n_assign, dtype=jnp.int32) - start[e_sorted]
    n_blocks = n_assign // MOE_BLOCK + N_EXPERTS
    block_end = jnp.cumsum(pad_counts) // MOE_BLOCK
    block_e = jnp.minimum(jnp.searchsorted(block_end, jnp.arange(n_blocks), side='right'), N_EXPERTS - 1)
    buf = jnp.zeros((n_blocks * MOE_BLOCK, d), xt.dtype).at[dest].set(xt[tok_sorted])

    def expert_block(args):
        xblk, e = args
        return swiglu(xblk, w_gate[e], w_up[e], w_down[e])

    y_blocks = lax.map(expert_block, (buf.reshape(n_blocks, MOE_BLOCK, d), block_e))
    y_assign = y_blocks.reshape(-1, d)[dest] * gate_sorted[:, None].astype(xt.dtype)
    return jnp.zeros_like(xt).at[tok_sorted].add(y_assign)


def moe_ffn(x, lp):
    shared = swiglu(x, lp['sh_gate'], lp['sh_up'], lp['sh_down'])
    routed = lax.map(lambda xs: routed_experts(xs, lp['router_w'], lp['router_b'], lp['w_gate'], lp['w_up'], lp['w_down']), x)
    return shared + routed.astype(x.dtype)


def encoder_layer(x, mem, lp):
    alpha = (2.0 * DEPTH) ** 0.25
    x = layer_norm(alpha * x + parallel_mixer(x, lp), lp['ln1_g'], lp['ln1_b'])
    x = layer_norm(alpha * x + cross_attention(x, mem, lp), lp['ln2_g'], lp['ln2_b'])
    x = layer_norm(alpha * x + moe_ffn(x, lp), lp['ln3_g'], lp['ln3_b'])
    return x


def run_trunk(x, mem, ln_in_g, ln_in_b, layers):
    x = layer_norm(x, ln_in_g, ln_in_b)
    for l in range(DEPTH):
        x = encoder_layer(x, mem, layers[l])
    return x


def setup_inputs(seed: int = 0) -> dict:
    key = jax.random.key(seed)
    ks = iter(jax.random.split(key, 48))
    f32 = jnp.float32
    beta = (8.0 * DEPTH) ** -0.25
    L = DEPTH

    def nrm(shape, scale):
        return jax.random.normal(next(ks), shape, f32) * scale

    def gain(shape):
        return 1.0 + nrm(shape, 0.01)

    in_col_scale = jnp.concatenate([jnp.ones((D_LRU,), f32), jnp.full((D_LRU,), beta, f32),
                                    jnp.ones((D_ATT + KV_DIM,), f32), jnp.full((KV_DIM,), beta, f32)])
    xa_col_scale = jnp.concatenate([jnp.ones((D_MODEL,), f32), jnp.full((D_MODEL,), beta, f32)])
    u = jax.random.uniform(next(ks), (L, 2, D_LRU), f32, 0.9, 0.999)
    a0 = u ** (1.0 / LRU_C)
    lru_lambda = jnp.log(a0) - jnp.log1p(-a0)
    return {
        'x_prompt': nrm((BATCH, SEQ, D_MODEL), 1.0),
        'x_sample': nrm((DEC_BATCH, DEC_SEQ, D_MODEL), 1.0),
        'mem_prompt': nrm((BATCH, N_MEM, D_MODEL), 1.0),
        'mem_sample': nrm((DEC_BATCH, N_MEM, D_MODEL), 1.0),
        'ln_in_g': gain((D_MODEL,)),
        'ln_in_b': nrm((D_MODEL,), 0.01),
        'w_in': nrm((L, D_MODEL, D_IN), D_MODEL ** -0.5) * in_col_scale,
        'conv_w': nrm((L, CONV_W, D_LRU), 0.5),
        'conv_b': nrm((L, D_LRU), 0.01),
        'lru_wa': nrm((L, 2, LRU_BLOCKS, LRU_BLK, LRU_BLK), LRU_BLK ** -0.5),
        'lru_ba': nrm((L, 2, D_LRU), 0.01),
        'lru_wx': nrm((L, 2, LRU_BLOCKS, LRU_BLK, LRU_BLK), LRU_BLK ** -0.5),
        'lru_bx': nrm((L, 2, D_LRU), 0.01),
        'lru_lambda': lru_lambda,
        'q_norm_g': gain((L, HEAD_DIM)),
        'k_norm_g': gain((L, HEAD_DIM)),
        'gn_lru_g': gain((L, D_LRU)),
        'gn_att_g': gain((L, D_ATT)),
        'w_out': nrm((L, D_LRU + D_ATT, D_MODEL), beta * (D_LRU + D_ATT) ** -0.5),
        'ln1_g': gain((L, D_MODEL)),
        'ln1_b': nrm((L, D_MODEL), 0.01),
        'xa_wq': nrm((L, D_MODEL, D_MODEL), D_MODEL ** -0.5),
        'xa_wkv': nrm((L, D_MODEL, 2 * D_MODEL), D_MODEL ** -0.5) * xa_col_scale,
        'xa_wo': nrm((L, D_MODEL, D_MODEL), beta * D_MODEL ** -0.5),
        'ln2_g': gain((L, D_MODEL)),
        'ln2_b': nrm((L, D_MODEL), 0.01),
        'router_w': nrm((L, D_MODEL, N_EXPERTS), D_MODEL ** -0.5),
        'router_b': nrm((L, N_EXPERTS), 0.01),
        'w_gate': nrm((L, N_EXPERTS, D_MODEL, D_EXPERT), beta * D_MODEL ** -0.5),
        'w_up': nrm((L, N_EXPERTS, D_MODEL, D_EXPERT), beta * D_MODEL ** -0.5),
        'w_down': nrm((L, N_EXPERTS, D_EXPERT, D_MODEL), beta * D_EXPERT ** -0.5),
        'sh_gate': nrm((L, D_MODEL, D_EXPERT), beta * D_MODEL ** -0.5),
        'sh_up': nrm((L, D_MODEL, D_EXPERT), beta * D_MODEL ** -0.5),
        'sh_down': nrm((L, D_EXPERT, D_MODEL), beta * D_EXPERT ** -0.5),
        'ln3_g': gain((L, D_MODEL)),
        'ln3_b': nrm((L, D_MODEL), 0.01),
    }


def reference(x_prompt, x_sample, mem_prompt, mem_sample, ln_in_g, ln_in_b, w_in, conv_w, conv_b,
              lru_wa, lru_ba, lru_wx, lru_bx, lru_lambda, q_norm_g, k_norm_g, gn_lru_g, gn_att_g, w_out,
              ln1_g, ln1_b, xa_wq, xa_wkv, xa_wo, ln2_g, ln2_b, router_w, router_b, w_gate, w_up, w_down,
              sh_gate, sh_up, sh_down, ln3_g, ln3_b):
    layers = []
    for l in range(DEPTH):
        layers.append(dict(
            w_in=w_in[l], conv_w=conv_w[l], conv_b=conv_b[l],
            lru_wa=lru_wa[l], lru_ba=lru_ba[l], lru_wx=lru_wx[l], lru_bx=lru_bx[l], lru_lambda=lru_lambda[l],
            q_norm_g=q_norm_g[l], k_norm_g=k_norm_g[l], gn_lru_g=gn_lru_g[l], gn_att_g=gn_att_g[l],
            w_out=w_out[l], ln1_g=ln1_g[l], ln1_b=ln1_b[l],
            xa_wq=xa_wq[l], xa_wkv=xa_wkv[l], xa_wo=xa_wo[l], ln2_g=ln2_g[l], ln2_b=ln2_b[l],
            router_w=router_w[l], router_b=router_b[l], w_gate=w_gate[l], w_up=w_up[l], w_down=w_down[l],
            sh_gate=sh_gate[l], sh_up=sh_up[l], sh_down=sh_down[l], ln3_g=ln3_g[l], ln3_b=ln3_b[l]))
    y_prompt = run_trunk(x_prompt, mem_prompt, ln_in_g, ln_in_b, layers)
    y_sample = run_trunk(x_sample, mem_sample, ln_in_g, ln_in_b, layers)
    return (y_prompt, y_sample)
```

```python
import functools

import jax
import jax.numpy as jnp
from jax import lax
from jax.experimental import pallas as pl
from jax.experimental.pallas import tpu as pltpu

F32 = jnp.float32
BF16 = jnp.bfloat16

D_MODEL = 1024
HEAD_DIM = 64
D_LRU = 512
LRU_BLOCKS = 8
LRU_C = 8.0
CONV_W = 4
CONV_LEFT = 2
ATT_HEADS = 8
KV_HEADS = 2
Q_GROUP = ATT_HEADS // KV_HEADS
D_ATT = ATT_HEADS * HEAD_DIM
KV_DIM = KV_HEADS * HEAD_DIM
D_IN = 2 * D_LRU + D_ATT + 2 * KV_DIM
GRID_W = 64
ROPE_THETA = 10000.0
ROPE_AXIS_FREQS = HEAD_DIM // 4
N_MEM = 256
X_HEADS = 4
X_HEAD_DIM = D_MODEL // X_HEADS
N_EXPERTS = 64
TOP_K = 8
N_GROUPS = 8
GROUP_SIZE = N_EXPERTS // N_GROUPS
TOPK_GROUPS = 4
D_EXPERT = 256
ROUTED_SCALE = 2.5
LN_EPS = 1e-5
RMS_EPS = 1e-6

LANES = 128
SUBLANES = 8
TOK_TILE = 512
LRU_TILE = 512
ATT_TQ = 256
ATT_TK = 512
MOE_ROWS = 256
VMEM_LIMIT = 48 * 1024 * 1024


def _cparams(*sem):
    return pltpu.CompilerParams(dimension_semantics=sem, vmem_limit_bytes=VMEM_LIMIT)


def _full(shape):
    return pl.BlockSpec(shape, lambda *_: (0,) * len(shape))


def _layer_norm(x, g, b):
    mu = jnp.mean(x, axis=-1, keepdims=True)
    xc = x - mu
    var = jnp.mean(xc * xc, axis=-1, keepdims=True)
    return xc * lax.rsqrt(var + LN_EPS) * g + b


def _row_rms(x, g):
    return x * lax.rsqrt(jnp.mean(x * x, axis=-1, keepdims=True) + RMS_EPS) * g


def _dot(a, b):
    return jnp.dot(a, b, preferred_element_type=F32)


def _dot_nt(a, b):
    return lax.dot_general(a, b, (((1,), (1,)), ((), ())), preferred_element_type=F32)


def _split_bf16(x):
    hi = x.astype(BF16)
    lo = (x - hi.astype(F32)).astype(BF16)
    return hi, lo


def _head_mean_square(z, avg_ref):
    hi, lo = _split_bf16(z * z)
    avg = avg_ref[...]
    return _dot(hi, avg) + _dot(lo, avg)


def _rope(x, cos, sin_signed):
    lane = lax.broadcasted_iota(jnp.int32, (1, LANES), 1)
    low_half = (lane % HEAD_DIM) < (HEAD_DIM // 2)
    outs = []
    for c in range(x.shape[1] // LANES):
        xc = x[:, c * LANES:(c + 1) * LANES]
        partner = jnp.where(low_half, pltpu.roll(xc, LANES - HEAD_DIM // 2, 1), pltpu.roll(xc, HEAD_DIM // 2, 1))
        outs.append(xc * cos + partner * sin_signed)
    return outs[0] if len(outs) == 1 else jnp.concatenate(outs, axis=1)


def _in_proj_kernel(x_ref, lg_ref, lb_ref, w_ref, gq_ref, gk_ref, cos_ref, sin_ref, avgq_ref, avgk_ref,
                    g_out, xbr_out, q_out, k_out, v_out):
    xn = _layer_norm(x_ref[...], lg_ref[...], lb_ref[...])
    z = _dot(xn.astype(BF16), w_ref[...])
    g_out[...] = z[:, :D_LRU]
    xbr_out[...] = z[:, D_LRU:2 * D_LRU]
    zq = z[:, 2 * D_LRU:2 * D_LRU + D_ATT]
    zk = z[:, 2 * D_LRU + D_ATT:2 * D_LRU + D_ATT + KV_DIM]
    v_out[...] = z[:, 2 * D_LRU + D_ATT + KV_DIM:].astype(BF16)
    cos = cos_ref[...]
    sin = sin_ref[...]
    q = zq * lax.rsqrt(_head_mean_square(zq, avgq_ref) + RMS_EPS) * gq_ref[...]
    k = zk * lax.rsqrt(_head_mean_square(zk, avgk_ref) + RMS_EPS) * gk_ref[...]
    q_out[...] = (_rope(q, cos, sin) * (HEAD_DIM ** -0.5)).astype(BF16)
    k_out[...] = _rope(k, cos, sin).astype(BF16)


def _in_proj(x, seq, ln_g, ln_b, w_in_b, gq, gk, cos_t, sin_t, avgq, avgk):
    n = x.shape[0]
    t = TOK_TILE
    nseq = seq // t
    tok = lambda w: pl.BlockSpec((t, w), lambda i: (i, 0))
    pos = pl.BlockSpec((t, LANES), lambda i: (i % nseq, 0))
    return pl.pallas_call(
        _in_proj_kernel,
        grid=(n // t,),
        in_specs=[tok(D_MODEL), _full((1, D_MODEL)), _full((1, D_MODEL)), _full((D_MODEL, D_IN)),
                  _full((1, D_ATT)), _full((1, KV_DIM)), pos, pos, _full((D_ATT, D_ATT)), _full((KV_DIM, KV_DIM))],
        out_specs=[tok(D_LRU), tok(D_LRU), tok(D_ATT), tok(KV_DIM), tok(KV_DIM)],
        out_shape=[jax.ShapeDtypeStruct((n, D_LRU), F32), jax.ShapeDtypeStruct((n, D_LRU), F32),
                   jax.ShapeDtypeStruct((n, D_ATT), BF16), jax.ShapeDtypeStruct((n, KV_DIM), BF16),
                   jax.ShapeDtypeStruct((n, KV_DIM), BF16)],
        compiler_params=_cparams("parallel"),
        name="in_proj",
    )(x, ln_g, ln_b, w_in_b, gq, gk, cos_t, sin_t, avgq, avgk)


def _softplus(x):
    return jnp.maximum(x, 0.0) + jnp.log1p(jnp.exp(-jnp.abs(x)))


def _lru_kernel(xp_ref, x_ref, xn_ref, cw_ref, cb_ref, w_ref, ba_ref, bx_ref, lam_ref, h_ref,
                a_s, b_s, c_s, *, tb, nt):
    d = pl.program_id(1)
    i = pl.program_id(2)
    tbi = jnp.where(d == 0, i, nt - 1 - i)
    x = x_ref[...]
    prev = jnp.where(tbi > 0, xp_ref[...], 0.0)
    nxt = jnp.where(tbi < nt - 1, xn_ref[...], 0.0)
    xe = jnp.concatenate([prev, x, nxt], axis=0)
    cw = cw_ref[...]
    o = SUBLANES - CONV_LEFT
    xc = xe[o:o + tb] * cw[0:1]
    for k in range(1, CONV_W):
        xc = xc + xe[o + k:o + k + tb] * cw[k:k + 1]
    xc = xc + cb_ref[...]
    rg = _dot(xc.astype(BF16), w_ref[...])
    r = jax.nn.sigmoid(rg[:, :D_LRU] + ba_ref[...])
    ig = jax.nn.sigmoid(rg[:, D_LRU:] + bx_ref[...])
    log_a = -LRU_C * r * _softplus(-lam_ref[...])
    a = jnp.exp(log_a)
    b = jnp.sqrt(-jnp.tanh(log_a) * (a * a + 1.0)) * (ig * xc)

    row = lax.broadcasted_iota(jnp.int32, (tb, D_LRU), 0) % SUBLANES

    @pl.when(i == 0)
    def _():
        c_s[...] = jnp.zeros_like(c_s)

    ngroups = tb // SUBLANES

    def scan(reverse):
        aa, bb = a, b
        for k in (1, 2, 4):
            if reverse:
                valid = row < SUBLANES - k
                shift = tb - k
            else:
                valid = row >= k
                shift = k
            a_sh = jnp.where(valid, pltpu.roll(aa, shift, 0), 1.0)
            b_sh = jnp.where(valid, pltpu.roll(bb, shift, 0), 0.0)
            bb = aa * b_sh + bb
            aa = aa * a_sh
        a_s[...] = aa
        b_s[...] = bb

        def body(gi, c):
            g = (ngroups - 1 - gi) if reverse else gi
            sl = pl.ds(pl.multiple_of(g * SUBLANES, SUBLANES), SUBLANES)
            h = a_s[sl, :] * c + b_s[sl, :]
            h_ref[sl, :] = h
            edge = h[0:1, :] if reverse else h[SUBLANES - 1:SUBLANES, :]
            return jnp.broadcast_to(edge, (SUBLANES, D_LRU))

        c_s[...] = lax.fori_loop(0, ngroups, body, c_s[...], unroll=8)

    @pl.when(d == 0)
    def _():
        scan(False)

    @pl.when(d == 1)
    def _():
        scan(True)


def _lru(xbr, conv_w, conv_b, w_gates, ba, bx, lam):
    bsz, seq, _ = xbr.shape
    tb = LRU_TILE
    nt = seq // tb
    r8 = tb // SUBLANES
    nb8 = seq // SUBLANES

    def tblk(d, i):
        return jnp.where(d == 0, i, nt - 1 - i)

    cur = pl.BlockSpec((None, tb, D_LRU), lambda b, d, i: (b, tblk(d, i), 0))
    prv = pl.BlockSpec((None, SUBLANES, D_LRU), lambda b, d, i: (b, jnp.maximum(tblk(d, i) * r8 - 1, 0), 0))
    nxt = pl.BlockSpec((None, SUBLANES, D_LRU), lambda b, d, i: (b, jnp.minimum((tblk(d, i) + 1) * r8, nb8 - 1), 0))
    per_dir = lambda shape: pl.BlockSpec((None,) + shape, lambda b, d, i: (d,) + (0,) * len(shape))
    return pl.pallas_call(
        functools.partial(_lru_kernel, tb=tb, nt=nt),
        grid=(bsz, 2, nt),
        in_specs=[prv, cur, nxt, _full((CONV_W, D_LRU)), _full((1, D_LRU)),
                  per_dir((D_LRU, 2 * D_LRU)), per_dir((1, D_LRU)), per_dir((1, D_LRU)), per_dir((1, D_LRU))],
        out_specs=pl.BlockSpec((None, None, tb, D_LRU), lambda b, d, i: (d, b, tblk(d, i), 0)),
        out_shape=jax.ShapeDtypeStruct((2, bsz, seq, D_LRU), F32),
        scratch_shapes=[pltpu.VMEM((tb, D_LRU), F32), pltpu.VMEM((tb, D_LRU), F32),
                        pltpu.VMEM((SUBLANES, D_LRU), F32)],
        compiler_params=_cparams("parallel", "arbitrary", "arbitrary"),
        name="lru",
    )(xbr, xbr, xbr, conv_w, conv_b, w_gates, ba, bx, lam)


def _attn_kernel(q_ref, k_ref, v_ref, o_ref, *, tq, tk, seq):
    lane = lax.broadcasted_iota(jnp.int32, (1, LANES), 1)
    low = lane < HEAD_DIM
    nk = seq // tk
    for c in range(D_ATT // LANES):
        qc = q_ref[:, c * LANES:(c + 1) * LANES]
        zero = jnp.zeros_like(qc)
        qs = jnp.concatenate([jnp.where(low, qc, zero), jnp.where(low, zero, qc)], axis=0)

        def body(j, carry):
            m, l, acc = carry
            sl = pl.ds(pl.multiple_of(j * tk, tk), tk)
            s = _dot_nt(qs, k_ref[sl, :])
            m_new = jnp.maximum(m, jnp.max(s, axis=1, keepdims=True))
            alpha = jnp.exp(m - m_new)
            p = jnp.exp(s - m_new)
            l = alpha * l + jnp.sum(p, axis=1, keepdims=True)
            acc = alpha * acc + _dot(p.astype(BF16), v_ref[sl, :])
            return m_new, l, acc

        m0 = jnp.full((2 * tq, 1), -jnp.inf, F32)
        l0 = jnp.zeros((2 * tq, 1), F32)
        a0 = jnp.zeros((2 * tq, LANES), F32)
        _, l, acc = lax.fori_loop(0, nk, body, (m0, l0, a0))
        o = acc / l
        o_ref[:, c * LANES:(c + 1) * LANES] = jnp.where(low, o[:tq], o[tq:])


def _attn(q, k, v):
    bsz, seq, _ = q.shape
    tq, tk = ATT_TQ, ATT_TK
    kv = pl.BlockSpec((None, seq, KV_DIM), lambda b, i: (b, 0, 0))
    qo = pl.BlockSpec((None, tq, D_ATT), lambda b, i: (b, i, 0))
    return pl.pallas_call(
        functools.partial(_attn_kernel, tq=tq, tk=tk, seq=seq),
        grid=(bsz, seq // tq),
        in_specs=[qo, kv, kv],
        out_specs=qo,
        out_shape=jax.ShapeDtypeStruct((bsz, seq, D_ATT), F32),
        compiler_params=_cparams("parallel", "parallel"),
        name="attn",
    )(q, k, v)


def _kv_mem_kernel(m_ref, w_ref, k_out, v_out):
    kv = _dot(m_ref[...].astype(BF16), w_ref[...])
    k_out[...] = kv[:, :D_MODEL].astype(BF16)
    v_out[...] = kv[:, D_MODEL:].astype(BF16)


def _kv_mem(mem, wkv_b):
    rows = mem.shape[0]
    blk = pl.BlockSpec((N_MEM, D_MODEL), lambda i: (i, 0))
    return pl.pallas_call(
        _kv_mem_kernel,
        grid=(rows // N_MEM,),
        in_specs=[blk, _full((D_MODEL, 2 * D_MODEL))],
        out_specs=[blk, blk],
        out_shape=[jax.ShapeDtypeStruct((rows, D_MODEL), BF16)] * 2,
        compiler_params=_cparams("parallel"),
        name="kv_mem",
    )(mem, wkv_b)


def _post_kernel(x_ref, g_ref, hf_ref, hb_ref, ya_ref, km_ref, vm_ref,
                 lg_ref, lb_ref, gl_ref, ga_ref, wo1_ref, wo2_ref, l1g_ref, l1b_ref,
                 wq_ref, wxo_ref, l2g_ref, l2b_ref, x2_out, x2b_out, *, alpha):
    x0 = _layer_norm(x_ref[...], lg_ref[...], lb_ref[...])
    y_lru = (hf_ref[...] + hb_ref[...]) * jax.nn.gelu(g_ref[...], approximate=True)
    y_lru = _row_rms(y_lru, gl_ref[...])
    y_att = _row_rms(ya_ref[...], ga_ref[...])
    y = _dot(y_lru.astype(BF16), wo1_ref[...]) + _dot(y_att.astype(BF16), wo2_ref[...])
    x1 = _layer_norm(alpha * x0 + y, l1g_ref[...], l1b_ref[...])
    xq = _dot(x1.astype(BF16), wq_ref[...]).astype(BF16)
    heads = []
    for h in range(X_HEADS):
        sl = slice(h * X_HEAD_DIM, (h + 1) * X_HEAD_DIM)
        s = _dot_nt(xq[:, sl], km_ref[:, sl]) * (X_HEAD_DIM ** -0.5)
        e = jnp.exp(s - jnp.max(s, axis=1, keepdims=True))
        o = _dot(e.astype(BF16), vm_ref[:, sl]) / jnp.sum(e, axis=1, keepdims=True)
        heads.append(o.astype(BF16))
    o = jnp.concatenate(heads, axis=1)
    x2 = _layer_norm(alpha * x1 + _dot(o, wxo_ref[...]), l2g_ref[...], l2b_ref[...])
    x2_out[...] = x2
    x2b_out[...] = x2.astype(BF16)


def _post(x, g, h, y_att, kmem, vmem, seq, alpha, ln_g, ln_b, gl, ga, wo1, wo2, l1g, l1b, wq, wxo, l2g, l2b):
    n = x.shape[0]
    t = TOK_TILE
    nseq = seq // t
    tok = lambda w: pl.BlockSpec((t, w), lambda i: (i, 0))
    hdir = lambda d: pl.BlockSpec((None, t, D_LRU), lambda i: (d, i, 0))
    mem = pl.BlockSpec((N_MEM, D_MODEL), lambda i: (i // nseq, 0))
    vec = _full((1, D_MODEL))
    half = _full((1, D_LRU))
    sq = _full((D_MODEL, D_MODEL))
    return pl.pallas_call(
        functools.partial(_post_kernel, alpha=alpha),
        grid=(n // t,),
        in_specs=[tok(D_MODEL), tok(D_LRU), hdir(0), hdir(1), tok(D_ATT), mem, mem,
                  vec, vec, half, half, _full((D_LRU, D_MODEL)), _full((D_ATT, D_MODEL)), vec, vec,
                  sq, sq, vec, vec],
        out_specs=[tok(D_MODEL), tok(D_MODEL)],
        out_shape=[jax.ShapeDtypeStruct((n, D_MODEL), F32), jax.ShapeDtypeStruct((n, D_MODEL), BF16)],
        compiler_params=_cparams("parallel"),
        name="post",
    )(x, g, h, h, y_att, kmem, vmem, ln_g, ln_b, gl, ga, wo1, wo2, l1g, l1b, wq, wxo, l2g, l2b)


def _first_max(v, n):
    rows = lax.broadcasted_iota(jnp.int32, v.shape, 0)
    m = jnp.max(v, axis=0, keepdims=True)
    idx = jnp.min(jnp.where(v == m, rows, n), axis=0, keepdims=True)
    return m, idx, rows == idx


def _router_kernel(x_ref, xb_ref, rwh_ref, rwl_ref, rb_ref, tri_ref, sg_ref, su_ref, sd_ref,
                   eidx_out, gate_out, rank_out, cnt_out, base_out, cnt_s, *, alpha):
    step = pl.program_id(0)

    @pl.when(step == 0)
    def _():
        cnt_s[...] = jnp.zeros_like(cnt_s)

    x = x_ref[...]
    xh, xl = _split_bf16(x)
    logits = _dot_nt(rwh_ref[...], xh) + _dot_nt(rwh_ref[...], xl) + _dot_nt(rwl_ref[...], xh)
    scores = jax.nn.sigmoid(logits)
    biased = scores + rb_ref[...]
    t = x.shape[0]
    neg = -jnp.inf

    gscore = []
    for g in range(N_GROUPS):
        vg = biased[g * GROUP_SIZE:(g + 1) * GROUP_SIZE, :]
        m1, _, hit = _first_max(vg, GROUP_SIZE)
        m2 = jnp.max(jnp.where(hit, neg, vg), axis=0, keepdims=True)
        gscore.append(m1 + m2)
    gs = jnp.concatenate(gscore, axis=0)
    gsel = jnp.zeros(gs.shape, jnp.bool_)
    for _ in range(TOPK_GROUPS):
        _, _, hit = _first_max(gs, N_GROUPS)
        gsel = jnp.logical_or(gsel, hit)
        gs = jnp.where(hit, neg, gs)
    emask = jnp.concatenate(
        [jnp.broadcast_to(gsel[g:g + 1, :], (GROUP_SIZE, t)) for g in range(N_GROUPS)], axis=0)
    masked = jnp.where(emask, biased, neg)

    idxs, gates, hits = [], [], []
    sel = jnp.zeros(masked.shape, jnp.bool_)
    for _ in range(TOP_K):
        _, idx, hit = _first_max(masked, N_EXPERTS)
        idxs.append(idx)
        hits.append(hit)
        gates.append(jnp.sum(jnp.where(hit, scores, 0.0), axis=0, keepdims=True))
        sel = jnp.logical_or(sel, hit)
        masked = jnp.where(hit, neg, masked)
    gsum = gates[0]
    for gk in gates[1:]:
        gsum = gsum + gk
    gate = jnp.concatenate(gates, axis=0) / gsum * ROUTED_SCALE

    self = jnp.where(sel, 1.0, 0.0)
    before = _dot(self.astype(BF16), tri_ref[...]) + cnt_s[...]
    ranks = [jnp.sum(jnp.where(hit, before, 0.0), axis=0, keepdims=True) for hit in hits]
    cnt = cnt_s[...] + jnp.sum(self, axis=1, keepdims=True)
    cnt_s[...] = cnt

    eidx_out[...] = jnp.concatenate(idxs, axis=0)
    gate_out[...] = gate
    rank_out[...] = jnp.concatenate(ranks, axis=0).astype(jnp.int32)
    cnt_out[...] = cnt.astype(jnp.int32)

    xb = xb_ref[...]
    hmid = jax.nn.silu(_dot(xb, sg_ref[...])) * _dot(xb, su_ref[...])
    base_out[...] = alpha * x + _dot(hmid.astype(BF16), sd_ref[...])


def _router(x2, x2b, rwh, rwl, rb, tri, sg, su, sd, alpha):
    n = x2.shape[0]
    t = TOK_TILE
    tok = pl.BlockSpec((t, D_MODEL), lambda i: (i, 0))
    kt = pl.BlockSpec((TOP_K, t), lambda i: (0, i))
    return pl.pallas_call(
        functools.partial(_router_kernel, alpha=alpha),
        grid=(n // t,),
        in_specs=[tok, tok, _full((N_EXPERTS, D_MODEL)), _full((N_EXPERTS, D_MODEL)), _full((N_EXPERTS, 1)),
                  _full((t, t)), _full((D_MODEL, D_EXPERT)), _full((D_MODEL, D_EXPERT)), _full((D_EXPERT, D_MODEL))],
        out_specs=[kt, kt, kt, _full((N_EXPERTS, 1)), tok],
        out_shape=[jax.ShapeDtypeStruct((TOP_K, n), jnp.int32), jax.ShapeDtypeStruct((TOP_K, n), F32),
                   jax.ShapeDtypeStruct((TOP_K, n), jnp.int32), jax.ShapeDtypeStruct((N_EXPERTS, 1), jnp.int32),
                   jax.ShapeDtypeStruct((n, D_MODEL), F32)],
        scratch_shapes=[pltpu.VMEM((N_EXPERTS, 1), F32)],
        compiler_params=_cparams("arbitrary"),
        name="router",
    )(x2, x2b, rwh, rwl, rb, tri, sg, su, sd)


def _experts_kernel(be_ref, nv_ref, xs_ref, wg_ref, wu_ref, wd_ref, y_out):
    @pl.when(pl.program_id(0) < nv_ref[0])
    def _():
        xs = xs_ref[...]
        hmid = jax.nn.silu(_dot(xs, wg_ref[...])) * _dot(xs, wu_ref[...])
        y_out[...] = _dot(hmid.astype(BF16), wd_ref[...])

    @pl.when(pl.program_id(0) >= nv_ref[0])
    def _():
        y_out[...] = jnp.zeros_like(y_out)


def _experts(block_e, nvalid, xs, wg, wu, wd):
    rows = xs.shape[0]
    r = MOE_ROWS
    return pl.pallas_call(
        _experts_kernel,
        grid_spec=pltpu.PrefetchScalarGridSpec(
            num_scalar_prefetch=2,
            grid=(rows // r,),
            in_specs=[pl.BlockSpec((r, D_MODEL), lambda i, be, nv: (i, 0)),
                      pl.BlockSpec((None, D_MODEL, D_EXPERT), lambda i, be, nv: (be[i], 0, 0)),
                      pl.BlockSpec((None, D_MODEL, D_EXPERT), lambda i, be, nv: (be[i], 0, 0)),
                      pl.BlockSpec((None, D_EXPERT, D_MODEL), lambda i, be, nv: (be[i], 0, 0))],
            out_specs=pl.BlockSpec((r, D_MODEL), lambda i, be, nv: (i, 0)),
        ),
        out_shape=jax.ShapeDtypeStruct((rows, D_MODEL), F32),
        compiler_params=_cparams("arbitrary"),
        name="experts",
    )(block_e, nvalid, xs, wg, wu, wd)


def _combine_kernel(base_ref, yg_ref, gate_ref, lg_ref, lb_ref, o_ref):
    acc = base_ref[...]
    gate = gate_ref[...]
    routed = yg_ref[0] * gate[:, 0:1]
    for k in range(1, TOP_K):
        routed = routed + yg_ref[k] * gate[:, k:k + 1]
    o_ref[...] = _layer_norm(acc + routed, lg_ref[...], lb_ref[...])


def _combine(base, yg, gate_t, lg, lb):
    n = base.shape[0]
    t = TOK_TILE
    tok = pl.BlockSpec((t, D_MODEL), lambda i: (i, 0))
    return pl.pallas_call(
        _combine_kernel,
        grid=(n // t,),
        in_specs=[tok, pl.BlockSpec((TOP_K, t, D_MODEL), lambda i: (0, i, 0)),
                  pl.BlockSpec((t, TOP_K), lambda i: (i, 0)), _full((1, D_MODEL)), _full((1, D_MODEL))],
        out_specs=tok,
        out_shape=jax.ShapeDtypeStruct((n, D_MODEL), F32),
        compiler_params=_cparams("parallel"),
        name="combine",
    )(base, yg, gate_t, lg, lb)


def _q_lane_perm():
    order = []
    for c in range(Q_GROUP):
        for g in range(KV_HEADS):
            order.append(g * Q_GROUP + c)
    idx = []
    for h in order:
        idx.extend(range(h * HEAD_DIM, (h + 1) * HEAD_DIM))
    return jnp.asarray(idx, jnp.int32)


def _block_diag(w):
    nb, bi, bo = w.shape
    eye = jnp.eye(nb, dtype=w.dtype)
    return (eye[:, None, :, None] * w[:, :, None, :]).reshape(nb * bi, nb * bo)


def _head_avg(width):
    h = jnp.arange(width) // HEAD_DIM
    return jnp.where(h[:, None] == h[None, :], 1.0 / HEAD_DIM, 0.0).astype(BF16)


def _rope_tables(seq):
    rows = seq // GRID_W
    row_id, col_id = jnp.meshgrid(jnp.arange(rows), jnp.arange(GRID_W), indexing='ij')
    row_id = row_id.reshape(-1).astype(F32)
    col_id = col_id.reshape(-1).astype(F32)
    inv_freq = ROPE_THETA ** (-jnp.arange(ROPE_AXIS_FREQS, dtype=F32) / ROPE_AXIS_FREQS)
    ang = jnp.concatenate([row_id[:, None] * inv_freq, col_id[:, None] * inv_freq], axis=-1)
    cos, sin = jnp.cos(ang), jnp.sin(ang)
    reps = LANES // HEAD_DIM
    return jnp.tile(jnp.concatenate([cos, cos], axis=1), (1, reps)), jnp.tile(jnp.concatenate([-sin, sin], axis=1), (1, reps))


def _prep_layer(p):
    perm = _q_lane_perm()
    w_in = p['w_in']
    q0 = 2 * D_LRU
    w_in = jnp.concatenate([w_in[:, :q0], w_in[:, q0:q0 + D_ATT][:, perm], w_in[:, q0 + D_ATT:]], axis=1)
    w_out = p['w_out']
    row = lambda v: v.reshape(1, -1)
    rw_t = p['router_w'].T
    rwh = rw_t.astype(BF16)
    return dict(
        w_in=w_in.astype(BF16),
        gq=jnp.tile(p['q_norm_g'], ATT_HEADS).reshape(1, D_ATT),
        gk=jnp.tile(p['k_norm_g'], KV_HEADS).reshape(1, KV_DIM),
        conv_w=p['conv_w'], conv_b=row(p['conv_b']),
        w_gates=jnp.stack([jnp.concatenate([_block_diag(p['lru_wa'][d]), _block_diag(p['lru_wx'][d])], axis=1)
                           for d in range(2)]).astype(BF16),
        ba=p['lru_ba'].reshape(2, 1, D_LRU), bx=p['lru_bx'].reshape(2, 1, D_LRU),
        lam=p['lru_lambda'].reshape(2, 1, D_LRU),
        gl=row(p['gn_lru_g']), ga=row(p['gn_att_g'][perm]),
        wo1=w_out[:D_LRU].astype(BF16), wo2=w_out[D_LRU:][perm].astype(BF16),
        l1g=row(p['ln1_g']), l1b=row(p['ln1_b']),
        wq=p['xa_wq'].astype(BF16), wkv=p['xa_wkv'].astype(BF16), wxo=p['xa_wo'].astype(BF16),
        l2g=row(p['ln2_g']), l2b=row(p['ln2_b']),
        rwh=rwh, rwl=(rw_t - rwh.astype(F32)).astype(BF16), rb=p['router_b'].reshape(N_EXPERTS, 1),
        wg=p['w_gate'].astype(BF16), wu=p['w_up'].astype(BF16), wd=p['w_down'].astype(BF16),
        sg=p['sh_gate'].astype(BF16), su=p['sh_up'].astype(BF16), sd=p['sh_down'].astype(BF16),
        l3g=row(p['ln3_g']), l3b=row(p['ln3_b']),
    )


def _moe(x2, x2b, lp, alpha, tri):
    n = x2.shape[0]
    eidx, gate, rank, counts, base = _router(x2, x2b, lp['rwh'], lp['rwl'], lp['rb'], tri,
                                             lp['sg'], lp['su'], lp['sd'], alpha)
    r = MOE_ROWS
    n_blocks = (n * TOP_K) // r + N_EXPERTS
    counts = counts.reshape(N_EXPERTS)
    pad_counts = ((counts + r - 1) // r) * r
    pad_end = jnp.cumsum(pad_counts)
    pad_start = pad_end - pad_counts
    dest = pad_start[eidx] + rank
    block_e = jnp.minimum(jnp.searchsorted(pad_end // r, jnp.arange(n_blocks), side='right'),
                          N_EXPERTS - 1).astype(jnp.int32)
    nvalid = (pad_end[-1] // r).astype(jnp.int32).reshape(1)
    flat = dest.reshape(-1)
    tok = jnp.broadcast_to(jnp.arange(n, dtype=jnp.int32)[None, :], (TOP_K, n)).reshape(-1)
    xs = jnp.zeros((n_blocks * r, D_MODEL), BF16).at[flat].set(x2b[tok], unique_indices=True)
    ys = _experts(block_e, nvalid, xs, lp['wg'], lp['wu'], lp['wd'])
    yg = ys[flat].reshape(TOP_K, n, D_MODEL)
    return base, yg, gate.T


def _layer(x, mem, lp, alpha, consts, first):
    bsz, seq, _ = x.shape
    n = bsz * seq
    xf = x.reshape(n, D_MODEL)
    cos_t, sin_t = _rope_tables(seq)
    g, xbr, q, k, v = _in_proj(xf, seq, consts['ln_g'], consts['ln_b'], lp['w_in'], lp['gq'], lp['gk'],
                               cos_t, sin_t, consts['avgq'], consts['avgk'])
    h = _lru(xbr.reshape(bsz, seq, D_LRU), lp['conv_w'], lp['conv_b'], lp['w_gates'], lp['ba'], lp['bx'], lp['lam'])
    y_att = _attn(q.reshape(bsz, seq, D_ATT), k.reshape(bsz, seq, KV_DIM), v.reshape(bsz, seq, KV_DIM))
    kmem, vmem = _kv_mem(mem.reshape(bsz * N_MEM, D_MODEL), lp['wkv'])
    x2, x2b = _post(xf, g, h.reshape(2, n, D_LRU), y_att.reshape(n, D_ATT), kmem, vmem, seq, alpha,
                    consts['ln_g'], consts['ln_b'], lp['gl'], lp['ga'], lp['wo1'], lp['wo2'], lp['l1g'], lp['l1b'],
                    lp['wq'], lp['wxo'], lp['l2g'], lp['l2b'])
    base, yg, gate_t = _moe(x2, x2b, lp, alpha, consts['tri'])
    out = _combine(base, yg, gate_t, lp['l3g'], lp['l3b'])
    return out.reshape(bsz, seq, D_MODEL)


def kernel(x_prompt, x_sample, mem_prompt, mem_sample, ln_in_g, ln_in_b, w_in, conv_w, conv_b, lru_wa, lru_ba, lru_wx, lru_bx, lru_lambda, q_norm_g, k_norm_g, gn_lru_g, gn_att_g, w_out, ln1_g, ln1_b, xa_wq, xa_wkv, xa_wo, ln2_g, ln2_b, router_w, router_b, w_gate, w_up, w_down, sh_gate, sh_up, sh_down, ln3_g, ln3_b):
    depth = w_in.shape[0]
    assert depth == 1, "the fused in_proj/post kernels assume the input LayerNorm feeds a single layer"
    alpha = (2.0 * depth) ** 0.25
    stacked = dict(w_in=w_in, conv_w=conv_w, conv_b=conv_b, lru_wa=lru_wa, lru_ba=lru_ba, lru_wx=lru_wx,
                   lru_bx=lru_bx, lru_lambda=lru_lambda, q_norm_g=q_norm_g, k_norm_g=k_norm_g, gn_lru_g=gn_lru_g,
                   gn_att_g=gn_att_g, w_out=w_out, ln1_g=ln1_g, ln1_b=ln1_b, xa_wq=xa_wq, xa_wkv=xa_wkv,
                   xa_wo=xa_wo, ln2_g=ln2_g, ln2_b=ln2_b, router_w=router_w, router_b=router_b, w_gate=w_gate,
                   w_up=w_up, w_down=w_down, sh_gate=sh_gate, sh_up=sh_up, sh_down=sh_down, ln3_g=ln3_g, ln3_b=ln3_b)
    lp = _prep_layer({name: val[0] for name, val in stacked.items()})
    tri = (jnp.arange(TOK_TILE)[:, None] < jnp.arange(TOK_TILE)[None, :]).astype(BF16)
    consts = dict(ln_g=ln_in_g.reshape(1, D_MODEL), ln_b=ln_in_b.reshape(1, D_MODEL),
                  avgq=_head_avg(D_ATT), avgk=_head_avg(KV_DIM), tri=tri)
    y_prompt = _layer(x_prompt, mem_prompt, lp, alpha, consts, True)
    y_sample = _layer(x_sample, mem_sample, lp, alpha, consts, False)
    return (y_prompt, y_sample)
```

```python
import functools

import jax
import jax.numpy as jnp
from jax import lax
from jax.experimental import pallas as pl
from jax.experimental.pallas import tpu as pltpu

F32 = jnp.float32
BF16 = jnp.bfloat16
U32 = jnp.uint32

D_MODEL = 1024
HEAD_DIM = 64
D_LRU = 512
LRU_BLOCKS = 8
LRU_C = 8.0
CONV_W = 4
CONV_LEFT = 2
ATT_HEADS = 8
KV_HEADS = 2
Q_GROUP = ATT_HEADS // KV_HEADS
D_ATT = ATT_HEADS * HEAD_DIM
KV_DIM = KV_HEADS * HEAD_DIM
D_IN = 2 * D_LRU + D_ATT + 2 * KV_DIM
GRID_W = 64
ROPE_THETA = 10000.0
ROPE_AXIS_FREQS = HEAD_DIM // 4
N_MEM = 256
X_HEADS = 4
X_HEAD_DIM = D_MODEL // X_HEADS
N_EXPERTS = 64
TOP_K = 8
N_GROUPS = 8
GROUP_SIZE = N_EXPERTS // N_GROUPS
TOPK_GROUPS = 4
D_EXPERT = 256
PACKED = D_MODEL // 2
ROUTED_SCALE = 2.5
LN_EPS = 1e-5
RMS_EPS = 1e-6

LANES = 128
SUBLANES = 8
TOK_TILE = 512
LRU_TILE = 512
ATT_TQ = 256
ATT_TK = 512
MOE_ROWS = 256
MOE_TOK = 256
VMEM_LIMIT = 48 * 1024 * 1024


def _cparams(*sem):
    return pltpu.CompilerParams(dimension_semantics=sem, vmem_limit_bytes=VMEM_LIMIT)


def _full(shape):
    return pl.BlockSpec(shape, lambda *_: (0,) * len(shape))


def _layer_norm(x, g, b):
    mu = jnp.mean(x, axis=-1, keepdims=True)
    xc = x - mu
    var = jnp.mean(xc * xc, axis=-1, keepdims=True)
    return xc * lax.rsqrt(var + LN_EPS) * g + b


def _row_rms(x, g):
    return x * lax.rsqrt(jnp.mean(x * x, axis=-1, keepdims=True) + RMS_EPS) * g


def _dot(a, b):
    return jnp.dot(a, b, preferred_element_type=F32)


def _dot_nt(a, b):
    return lax.dot_general(a, b, (((1,), (1,)), ((), ())), preferred_element_type=F32)


def _pack_halves(x):
    w = x.shape[1] // 2
    lo = lax.bitcast_convert_type(x[:, :w].astype(BF16).astype(F32), U32)
    hi = lax.bitcast_convert_type(x[:, w:].astype(BF16).astype(F32), U32)
    return (lo >> 16) | (hi & jnp.uint32(0xFFFF0000))


def _unpack_halves(w):
    lo = lax.bitcast_convert_type(w << 16, F32)
    hi = lax.bitcast_convert_type(w & jnp.uint32(0xFFFF0000), F32)
    return lo, hi


def _split_bf16(x):
    hi = x.astype(BF16)
    lo = (x - hi.astype(F32)).astype(BF16)
    return hi, lo


def _head_mean_square(z, avg_ref):
    hi, lo = _split_bf16(z * z)
    avg = avg_ref[...]
    return _dot(hi, avg) + _dot(lo, avg)


def _rope(x, cos, sin_signed):
    lane = lax.broadcasted_iota(jnp.int32, (1, LANES), 1)
    low_half = (lane % HEAD_DIM) < (HEAD_DIM // 2)
    outs = []
    for c in range(x.shape[1] // LANES):
        xc = x[:, c * LANES:(c + 1) * LANES]
        partner = jnp.where(low_half, pltpu.roll(xc, LANES - HEAD_DIM // 2, 1), pltpu.roll(xc, HEAD_DIM // 2, 1))
        outs.append(xc * cos + partner * sin_signed)
    return outs[0] if len(outs) == 1 else jnp.concatenate(outs, axis=1)


def _in_proj_kernel(x_ref, lg_ref, lb_ref, w_ref, gq_ref, gk_ref, cos_ref, sin_ref, avgq_ref, avgk_ref,
                    g_out, xbr_out, q_out, k_out, v_out):
    xn = _layer_norm(x_ref[...], lg_ref[...], lb_ref[...])
    z = _dot(xn.astype(BF16), w_ref[...])
    g_out[...] = z[:, :D_LRU]
    xbr_out[...] = z[:, D_LRU:2 * D_LRU]
    zq = z[:, 2 * D_LRU:2 * D_LRU + D_ATT]
    zk = z[:, 2 * D_LRU + D_ATT:2 * D_LRU + D_ATT + KV_DIM]
    v_out[...] = z[:, 2 * D_LRU + D_ATT + KV_DIM:].astype(BF16)
    cos = cos_ref[...]
    sin = sin_ref[...]
    q = zq * lax.rsqrt(_head_mean_square(zq, avgq_ref) + RMS_EPS) * gq_ref[...]
    k = zk * lax.rsqrt(_head_mean_square(zk, avgk_ref) + RMS_EPS) * gk_ref[...]
    q_out[...] = (_rope(q, cos, sin) * (HEAD_DIM ** -0.5)).astype(BF16)
    k_out[...] = _rope(k, cos, sin).astype(BF16)


def _in_proj(x, seq, ln_g, ln_b, w_in_b, gq, gk, cos_t, sin_t, avgq, avgk):
    n = x.shape[0]
    t = TOK_TILE
    nseq = seq // t
    tok = lambda w: pl.BlockSpec((t, w), lambda i: (i, 0))
    pos = pl.BlockSpec((t, LANES), lambda i: (i % nseq, 0))
    return pl.pallas_call(
        _in_proj_kernel,
        grid=(n // t,),
        in_specs=[tok(D_MODEL), _full((1, D_MODEL)), _full((1, D_MODEL)), _full((D_MODEL, D_IN)),
                  _full((1, D_ATT)), _full((1, KV_DIM)), pos, pos, _full((D_ATT, D_ATT)), _full((KV_DIM, KV_DIM))],
        out_specs=[tok(D_LRU), tok(D_LRU), tok(D_ATT), tok(KV_DIM), tok(KV_DIM)],
        out_shape=[jax.ShapeDtypeStruct((n, D_LRU), F32), jax.ShapeDtypeStruct((n, D_LRU), F32),
                   jax.ShapeDtypeStruct((n, D_ATT), BF16), jax.ShapeDtypeStruct((n, KV_DIM), BF16),
                   jax.ShapeDtypeStruct((n, KV_DIM), BF16)],
        compiler_params=_cparams("parallel"),
        name="in_proj",
    )(x, ln_g, ln_b, w_in_b, gq, gk, cos_t, sin_t, avgq, avgk)


def _softplus(x):
    return jnp.maximum(x, 0.0) + jnp.log1p(jnp.exp(-jnp.abs(x)))


def _lru_kernel(xp_ref, x_ref, xn_ref, cw_ref, cb_ref, w_ref, ba_ref, bx_ref, lam_ref, h_ref,
                a_s, b_s, c_s, *, tb, nt):
    d = pl.program_id(1)
    i = pl.program_id(2)
    tbi = jnp.where(d == 0, i, nt - 1 - i)
    x = x_ref[...]
    prev = jnp.where(tbi > 0, xp_ref[...], 0.0)
    nxt = jnp.where(tbi < nt - 1, xn_ref[...], 0.0)
    xe = jnp.concatenate([prev, x, nxt], axis=0)
    cw = cw_ref[...]
    o = SUBLANES - CONV_LEFT
    xc = xe[o:o + tb] * cw[0:1]
    for k in range(1, CONV_W):
        xc = xc + xe[o + k:o + k + tb] * cw[k:k + 1]
    xc = xc + cb_ref[...]
    rg = _dot(xc.astype(BF16), w_ref[...])
    r = jax.nn.sigmoid(rg[:, :D_LRU] + ba_ref[...])
    ig = jax.nn.sigmoid(rg[:, D_LRU:] + bx_ref[...])
    log_a = -LRU_C * r * _softplus(-lam_ref[...])
    a = jnp.exp(log_a)
    b = jnp.sqrt(-jnp.tanh(log_a) * (a * a + 1.0)) * (ig * xc)

    row = lax.broadcasted_iota(jnp.int32, (tb, D_LRU), 0) % SUBLANES

    @pl.when(i == 0)
    def _():
        c_s[...] = jnp.zeros_like(c_s)

    ngroups = tb // SUBLANES

    def scan(reverse):
        aa, bb = a, b
        for k in (1, 2, 4):
            if reverse:
                valid = row < SUBLANES - k
                shift = tb - k
            else:
                valid = row >= k
                shift = k
            a_sh = jnp.where(valid, pltpu.roll(aa, shift, 0), 1.0)
            b_sh = jnp.where(valid, pltpu.roll(bb, shift, 0), 0.0)
            bb = aa * b_sh + bb
            aa = aa * a_sh
        a_s[...] = aa
        b_s[...] = bb

        def body(gi, c):
            g = (ngroups - 1 - gi) if reverse else gi
            sl = pl.ds(pl.multiple_of(g * SUBLANES, SUBLANES), SUBLANES)
            h = a_s[sl, :] * c + b_s[sl, :]
            h_ref[sl, :] = h
            edge = h[0:1, :] if reverse else h[SUBLANES - 1:SUBLANES, :]
            return jnp.broadcast_to(edge, (SUBLANES, D_LRU))

        c_s[...] = lax.fori_loop(0, ngroups, body, c_s[...], unroll=8)

    @pl.when(d == 0)
    def _():
        scan(False)

    @pl.when(d == 1)
    def _():
        scan(True)


def _lru(xbr, conv_w, conv_b, w_gates, ba, bx, lam):
    bsz, seq, _ = xbr.shape
    tb = LRU_TILE
    nt = seq // tb
    r8 = tb // SUBLANES
    nb8 = seq // SUBLANES

    def tblk(d, i):
        return jnp.where(d == 0, i, nt - 1 - i)

    cur = pl.BlockSpec((None, tb, D_LRU), lambda b, d, i: (b, tblk(d, i), 0))
    prv = pl.BlockSpec((None, SUBLANES, D_LRU), lambda b, d, i: (b, jnp.maximum(tblk(d, i) * r8 - 1, 0), 0))
    nxt = pl.BlockSpec((None, SUBLANES, D_LRU), lambda b, d, i: (b, jnp.minimum((tblk(d, i) + 1) * r8, nb8 - 1), 0))
    per_dir = lambda shape: pl.BlockSpec((None,) + shape, lambda b, d, i: (d,) + (0,) * len(shape))
    return pl.pallas_call(
        functools.partial(_lru_kernel, tb=tb, nt=nt),
        grid=(bsz, 2, nt),
        in_specs=[prv, cur, nxt, _full((CONV_W, D_LRU)), _full((1, D_LRU)),
                  per_dir((D_LRU, 2 * D_LRU)), per_dir((1, D_LRU)), per_dir((1, D_LRU)), per_dir((1, D_LRU))],
        out_specs=pl.BlockSpec((None, None, tb, D_LRU), lambda b, d, i: (d, b, tblk(d, i), 0)),
        out_shape=jax.ShapeDtypeStruct((2, bsz, seq, D_LRU), F32),
        scratch_shapes=[pltpu.VMEM((tb, D_LRU), F32), pltpu.VMEM((tb, D_LRU), F32),
                        pltpu.VMEM((SUBLANES, D_LRU), F32)],
        compiler_params=_cparams("parallel", "arbitrary", "arbitrary"),
        name="lru",
    )(xbr, xbr, xbr, conv_w, conv_b, w_gates, ba, bx, lam)


def _attn_kernel(q_ref, k_ref, v_ref, o_ref, qs_s, m_s, l_s, acc_s, *, tq, tk, seq):
    lane = lax.broadcasted_iota(jnp.int32, (1, LANES), 1)
    low = lane < HEAD_DIM
    nchunk = D_ATT // LANES
    for c in range(nchunk):
        qc = q_ref[:, c * LANES:(c + 1) * LANES]
        zero = jnp.zeros_like(qc)
        qs_s[c, :tq, :] = jnp.where(low, qc, zero)
        qs_s[c, tq:, :] = jnp.where(low, zero, qc)
    m_s[...] = jnp.full(m_s.shape, -jnp.inf, F32)
    l_s[...] = jnp.zeros(l_s.shape, F32)
    acc_s[...] = jnp.zeros(acc_s.shape, F32)

    def body(j, carry):
        sl = pl.ds(pl.multiple_of(j * tk, tk), tk)
        kj = k_ref[sl, :]
        vj = v_ref[sl, :]
        for c in range(nchunk):
            s = _dot_nt(qs_s[c], kj)
            m_prev = m_s[c]
            m_next = jnp.maximum(m_prev, jnp.max(s, axis=1, keepdims=True))
            alpha = jnp.exp(m_prev - m_next)
            ps = [jnp.exp(s[:, t * LANES:(t + 1) * LANES] - m_next) for t in range(tk // LANES)]
            part = ps[0]
            for pt in ps[1:]:
                part = part + pt
            l_s[c] = alpha * l_s[c] + part
            p = jnp.concatenate(ps, axis=1).astype(BF16)
            acc_s[c] = alpha * acc_s[c] + _dot(p, vj)
            m_s[c] = m_next
        return carry

    lax.fori_loop(0, seq // tk, body, 0)
    for c in range(nchunk):
        o = acc_s[c] / jnp.sum(l_s[c], axis=1, keepdims=True)
        o_ref[:, c * LANES:(c + 1) * LANES] = jnp.where(low, o[:tq], o[tq:])


def _attn(q, k, v):
    bsz, seq, _ = q.shape
    tq, tk = ATT_TQ, ATT_TK
    kv = pl.BlockSpec((None, seq, KV_DIM), lambda b, i: (b, 0, 0))
    qo = pl.BlockSpec((None, tq, D_ATT), lambda b, i: (b, i, 0))
    stat = pltpu.VMEM((D_ATT // LANES, 2 * tq, LANES), F32)
    return pl.pallas_call(
        functools.partial(_attn_kernel, tq=tq, tk=tk, seq=seq),
        grid=(bsz, seq // tq),
        in_specs=[qo, kv, kv],
        out_specs=qo,
        out_shape=jax.ShapeDtypeStruct((bsz, seq, D_ATT), F32),
        scratch_shapes=[pltpu.VMEM((D_ATT // LANES, 2 * tq, LANES), BF16), stat, stat, stat],
        compiler_params=_cparams("parallel", "parallel"),
        name="attn",
    )(q, k, v)


def _kv_mem_kernel(m_ref, w_ref, k_out, v_out):
    kv = _dot(m_ref[...].astype(BF16), w_ref[...])
    k_out[...] = kv[:, :D_MODEL].astype(BF16)
    v_out[...] = kv[:, D_MODEL:].astype(BF16)


def _kv_mem(mem, wkv_b):
    rows = mem.shape[0]
    blk = pl.BlockSpec((N_MEM, D_MODEL), lambda i: (i, 0))
    return pl.pallas_call(
        _kv_mem_kernel,
        grid=(rows // N_MEM,),
        in_specs=[blk, _full((D_MODEL, 2 * D_MODEL))],
        out_specs=[blk, blk],
        out_shape=[jax.ShapeDtypeStruct((rows, D_MODEL), BF16)] * 2,
        compiler_params=_cparams("parallel"),
        name="kv_mem",
    )(mem, wkv_b)


def _post_kernel(x_ref, g_ref, hf_ref, hb_ref, ya_ref, km_ref, vm_ref,
                 lg_ref, lb_ref, gl_ref, ga_ref, wo1_ref, wo2_ref, l1g_ref, l1b_ref,
                 wq_ref, wxo_ref, l2g_ref, l2b_ref, x2_out, x2w_out, *, alpha):
    x0 = _layer_norm(x_ref[...], lg_ref[...], lb_ref[...])
    y_lru = (hf_ref[...] + hb_ref[...]) * jax.nn.gelu(g_ref[...], approximate=True)
    y_lru = _row_rms(y_lru, gl_ref[...])
    y_att = _row_rms(ya_ref[...], ga_ref[...])
    y = _dot(y_lru.astype(BF16), wo1_ref[...]) + _dot(y_att.astype(BF16), wo2_ref[...])
    x1 = _layer_norm(alpha * x0 + y, l1g_ref[...], l1b_ref[...])
    xq = _dot(x1.astype(BF16), wq_ref[...]).astype(BF16)
    heads = []
    for h in range(X_HEADS):
        sl = slice(h * X_HEAD_DIM, (h + 1) * X_HEAD_DIM)
        s = _dot_nt(xq[:, sl], km_ref[:, sl]) * (X_HEAD_DIM ** -0.5)
        e = jnp.exp(s - jnp.max(s, axis=1, keepdims=True))
        o = _dot(e.astype(BF16), vm_ref[:, sl]) / jnp.sum(e, axis=1, keepdims=True)
        heads.append(o.astype(BF16))
    o = jnp.concatenate(heads, axis=1)
    x2 = _layer_norm(alpha * x1 + _dot(o, wxo_ref[...]), l2g_ref[...], l2b_ref[...])
    x2_out[...] = x2
    x2w_out[...] = _pack_halves(x2)


def _post(x, g, h, y_att, kmem, vmem, seq, alpha, ln_g, ln_b, gl, ga, wo1, wo2, l1g, l1b, wq, wxo, l2g, l2b):
    n = x.shape[0]
    t = TOK_TILE
    nseq = seq // t
    tok = lambda w: pl.BlockSpec((t, w), lambda i: (i, 0))
    hdir = lambda d: pl.BlockSpec((None, t, D_LRU), lambda i: (d, i, 0))
    mem = pl.BlockSpec((N_MEM, D_MODEL), lambda i: (i // nseq, 0))
    vec = _full((1, D_MODEL))
    half = _full((1, D_LRU))
    sq = _full((D_MODEL, D_MODEL))
    return pl.pallas_call(
        functools.partial(_post_kernel, alpha=alpha),
        grid=(n // t,),
        in_specs=[tok(D_MODEL), tok(D_LRU), hdir(0), hdir(1), tok(D_ATT), mem, mem,
                  vec, vec, half, half, _full((D_LRU, D_MODEL)), _full((D_ATT, D_MODEL)), vec, vec,
                  sq, sq, vec, vec],
        out_specs=[tok(D_MODEL), tok(PACKED)],
        out_shape=[jax.ShapeDtypeStruct((n, D_MODEL), F32), jax.ShapeDtypeStruct((n, PACKED), U32)],
        compiler_params=_cparams("parallel"),
        name="post",
    )(x, g, h, h, y_att, kmem, vmem, ln_g, ln_b, gl, ga, wo1, wo2, l1g, l1b, wq, wxo, l2g, l2b)


def _first_max(v, n):
    rows = lax.broadcasted_iota(jnp.int32, v.shape, 0)
    m = jnp.max(v, axis=0, keepdims=True)
    idx = jnp.min(jnp.where(v == m, rows, n), axis=0, keepdims=True)
    return m, idx, rows == idx


def _router_kernel(x_ref, rwh_ref, rwl_ref, rb_ref, tri_ref, sg_ref, su_ref, sd_ref,
                   eidx_out, gate_out, rank_out, cnt_out, base_out, cnt_s, *, alpha):
    step = pl.program_id(0)

    @pl.when(step == 0)
    def _():
        cnt_s[...] = jnp.zeros_like(cnt_s)

    x = x_ref[...]
    xh, xl = _split_bf16(x)
    logits = _dot_nt(rwh_ref[...], xh) + _dot_nt(rwh_ref[...], xl) + _dot_nt(rwl_ref[...], xh)
    scores = jax.nn.sigmoid(logits)
    biased = scores + rb_ref[...]
    t = x.shape[0]
    neg = -jnp.inf

    gscore = []
    for g in range(N_GROUPS):
        vg = biased[g * GROUP_SIZE:(g + 1) * GROUP_SIZE, :]
        m1, _, hit = _first_max(vg, GROUP_SIZE)
        m2 = jnp.max(jnp.where(hit, neg, vg), axis=0, keepdims=True)
        gscore.append(m1 + m2)
    gs = jnp.concatenate(gscore, axis=0)
    gsel = jnp.zeros(gs.shape, jnp.bool_)
    for _ in range(TOPK_GROUPS):
        _, _, hit = _first_max(gs, N_GROUPS)
        gsel = jnp.logical_or(gsel, hit)
        gs = jnp.where(hit, neg, gs)
    emask = jnp.concatenate(
        [jnp.broadcast_to(gsel[g:g + 1, :], (GROUP_SIZE, t)) for g in range(N_GROUPS)], axis=0)
    masked = jnp.where(emask, biased, neg)

    idxs, gates, hits = [], [], []
    sel = jnp.zeros(masked.shape, jnp.bool_)
    for _ in range(TOP_K):
        _, idx, hit = _first_max(masked, N_EXPERTS)
        idxs.append(idx)
        hits.append(hit)
        gates.append(jnp.sum(jnp.where(hit, scores, 0.0), axis=0, keepdims=True))
        sel = jnp.logical_or(sel, hit)
        masked = jnp.where(hit, neg, masked)
    gsum = gates[0]
    for gk in gates[1:]:
        gsum = gsum + gk
    gate = jnp.concatenate(gates, axis=0) / gsum * ROUTED_SCALE

    chosen = jnp.where(sel, 1.0, 0.0)
    before = _dot(chosen.astype(BF16), tri_ref[...]) + cnt_s[...]
    ranks = [jnp.sum(jnp.where(hit, before, 0.0), axis=0, keepdims=True) for hit in hits]
    cnt = cnt_s[...] + jnp.sum(chosen, axis=1, keepdims=True)
    cnt_s[...] = cnt

    eidx_out[...] = jnp.concatenate(idxs, axis=0)
    gate_out[...] = gate
    rank_out[...] = jnp.concatenate(ranks, axis=0).astype(jnp.int32)
    cnt_out[...] = cnt.astype(jnp.int32)

    hmid = jax.nn.silu(_dot(xh, sg_ref[...])) * _dot(xh, su_ref[...])
    base_out[...] = alpha * x + _dot(hmid.astype(BF16), sd_ref[...])


def _router(x2, rwh, rwl, rb, tri, sg, su, sd, alpha):
    n = x2.shape[0]
    t = TOK_TILE
    tok = pl.BlockSpec((t, D_MODEL), lambda i: (i, 0))
    kt = pl.BlockSpec((TOP_K, t), lambda i: (0, i))
    return pl.pallas_call(
        functools.partial(_router_kernel, alpha=alpha),
        grid=(n // t,),
        in_specs=[tok, _full((N_EXPERTS, D_MODEL)), _full((N_EXPERTS, D_MODEL)), _full((N_EXPERTS, 1)),
                  _full((t, t)), _full((D_MODEL, D_EXPERT)), _full((D_MODEL, D_EXPERT)), _full((D_EXPERT, D_MODEL))],
        out_specs=[kt, kt, kt, _full((N_EXPERTS, 1)), tok],
        out_shape=[jax.ShapeDtypeStruct((TOP_K, n), jnp.int32), jax.ShapeDtypeStruct((TOP_K, n), F32),
                   jax.ShapeDtypeStruct((TOP_K, n), jnp.int32), jax.ShapeDtypeStruct((N_EXPERTS, 1), jnp.int32),
                   jax.ShapeDtypeStruct((n, D_MODEL), F32)],
        scratch_shapes=[pltpu.VMEM((N_EXPERTS, 1), F32)],
        compiler_params=_cparams("arbitrary"),
        name="router",
    )(x2, rwh, rwl, rb, tri, sg, su, sd)


def _row_copy(src, src_row, dst, dst_row, sem):
    return pltpu.make_async_copy(src.at[pl.ds(src_row, 1)], dst.at[pl.ds(dst_row, 1)], sem)


def _dispatch_kernel(zs_ref, zc_ref, nv_ref, dest_ref, x_ref, xs_out, zero_s, sem, zsem, *, t, n_blocks):
    @pl.when(pl.program_id(0) == 0)
    def _():
        zero_s[...] = jnp.zeros_like(zero_s)

        def per_expert(e, carry):
            def start(j, c):
                _row_copy(zero_s, 0, xs_out, zs_ref[e] + j, zsem).start()
                return c

            def wait(j, c):
                _row_copy(zero_s, 0, xs_out, zs_ref[e] + j, zsem).wait()
                return c

            lax.fori_loop(0, zc_ref[e], start, 0)
            lax.fori_loop(0, zc_ref[e], wait, 0)
            return carry

        lax.fori_loop(0, N_EXPERTS, per_expert, 0)

        def block_copy(b):
            return pltpu.make_async_copy(zero_s, xs_out.at[pl.ds(b * MOE_ROWS, MOE_ROWS)], zsem)

        def start_block(b, c):
            block_copy(b).start()
            return c

        def wait_block(b, c):
            block_copy(b).wait()
            return c

        lax.fori_loop(nv_ref[0], n_blocks, start_block, 0)
        lax.fori_loop(nv_ref[0], n_blocks, wait_block, 0)

    def issue(tok, carry):
        for k in range(TOP_K):
            _row_copy(x_ref, tok, xs_out, dest_ref[k, tok], sem).start(priority=k % 2)
        return carry

    lax.fori_loop(0, t, issue, 0, unroll=4)
    for k in range(TOP_K):
        pltpu.make_async_copy(x_ref, xs_out.at[pl.ds(0, t)], sem).wait()


def _dispatch(zstart, zcount, nvalid, dest, x2w, n_blocks):
    n = x2w.shape[0]
    t = MOE_TOK
    return pl.pallas_call(
        functools.partial(_dispatch_kernel, t=t, n_blocks=n_blocks),
        grid_spec=pltpu.PrefetchScalarGridSpec(
            num_scalar_prefetch=3,
            grid=(n // t,),
            in_specs=[pl.BlockSpec((TOP_K, t), lambda i, zs, zc, nv: (0, i), memory_space=pltpu.SMEM),
                      pl.BlockSpec((t, PACKED), lambda i, zs, zc, nv: (i, 0))],
            out_specs=pl.BlockSpec(memory_space=pl.ANY),
            scratch_shapes=[pltpu.VMEM((MOE_ROWS, PACKED), U32), pltpu.SemaphoreType.DMA, pltpu.SemaphoreType.DMA],
        ),
        out_shape=jax.ShapeDtypeStruct((n_blocks * MOE_ROWS, PACKED), U32),
        compiler_params=_cparams("arbitrary"),
        name="dispatch",
    )(zstart, zcount, nvalid, dest, x2w)


def _experts_kernel(be_ref, nv_ref, xs_ref, wg_ref, wu_ref, wd_ref, y_out):
    @pl.when(pl.program_id(0) < nv_ref[0])
    def _():
        lo, hi = _unpack_halves(xs_ref[...])
        lo = lo.astype(BF16)
        hi = hi.astype(BF16)
        gate = _dot(lo, wg_ref[:PACKED, :]) + _dot(hi, wg_ref[PACKED:, :])
        up = _dot(lo, wu_ref[:PACKED, :]) + _dot(hi, wu_ref[PACKED:, :])
        hmid = jax.nn.silu(gate) * up
        y_out[...] = _pack_halves(_dot(hmid.astype(BF16), wd_ref[...]))

    @pl.when(pl.program_id(0) >= nv_ref[0])
    def _():
        y_out[...] = jnp.zeros_like(y_out)


def _experts(block_e, nvalid, xs, wg, wu, wd):
    rows = xs.shape[0]
    r = MOE_ROWS
    blk = pl.BlockSpec((r, PACKED), lambda i, be, nv: (i, 0))
    return pl.pallas_call(
        _experts_kernel,
        grid_spec=pltpu.PrefetchScalarGridSpec(
            num_scalar_prefetch=2,
            grid=(rows // r,),
            in_specs=[blk,
                      pl.BlockSpec((None, D_MODEL, D_EXPERT), lambda i, be, nv: (be[i], 0, 0)),
                      pl.BlockSpec((None, D_MODEL, D_EXPERT), lambda i, be, nv: (be[i], 0, 0)),
                      pl.BlockSpec((None, D_EXPERT, D_MODEL), lambda i, be, nv: (be[i], 0, 0))],
            out_specs=blk,
        ),
        out_shape=jax.ShapeDtypeStruct((rows, PACKED), U32),
        compiler_params=_cparams("arbitrary"),
        name="experts",
    )(block_e, nvalid, xs, wg, wu, wd)


def _combine_kernel(dest_ref, base_ref, gate_ref, lg_ref, lb_ref, ys_ref, o_ref, buf, sem, *, t):
    def issue(tok, carry):
        for k in range(TOP_K):
            _row_copy(ys_ref, dest_ref[k, tok], buf.at[k], tok, sem).start(priority=k % 2)
        return carry

    lax.fori_loop(0, t, issue, 0, unroll=4)
    for k in range(TOP_K):
        pltpu.make_async_copy(ys_ref.at[pl.ds(0, t)], buf.at[k], sem).wait()
    gate = gate_ref[...]
    acc = base_ref[...]
    routed = None
    for k in range(TOP_K):
        lo, hi = _unpack_halves(buf[k])
        yk = jnp.concatenate([lo, hi], axis=1) * gate[:, k:k + 1]
        routed = yk if routed is None else routed + yk
    o_ref[...] = _layer_norm(acc + routed, lg_ref[...], lb_ref[...])


def _combine(dest, base, gate_t, lg, lb, ys):
    n = base.shape[0]
    t = MOE_TOK
    tok = pl.BlockSpec((t, D_MODEL), lambda i: (i, 0))
    return pl.pallas_call(
        functools.partial(_combine_kernel, t=t),
        grid=(n // t,),
        in_specs=[pl.BlockSpec((TOP_K, t), lambda i: (0, i), memory_space=pltpu.SMEM), tok,
                  pl.BlockSpec((t, TOP_K), lambda i: (i, 0)), _full((1, D_MODEL)), _full((1, D_MODEL)),
                  pl.BlockSpec(memory_space=pl.ANY)],
        out_specs=tok,
        out_shape=jax.ShapeDtypeStruct((n, D_MODEL), F32),
        scratch_shapes=[pltpu.VMEM((TOP_K, t, PACKED), U32), pltpu.SemaphoreType.DMA],
        compiler_params=_cparams("arbitrary"),
        name="combine",
    )(dest, base, gate_t, lg, lb, ys)


def _q_lane_perm():
    order = []
    for c in range(Q_GROUP):
        for g in range(KV_HEADS):
            order.append(g * Q_GROUP + c)
    idx = []
    for h in order:
        idx.extend(range(h * HEAD_DIM, (h + 1) * HEAD_DIM))
    return jnp.asarray(idx, jnp.int32)


def _block_diag(w):
    nb, bi, bo = w.shape
    eye = jnp.eye(nb, dtype=w.dtype)
    return (eye[:, None, :, None] * w[:, :, None, :]).reshape(nb * bi, nb * bo)


def _head_avg(width):
    h = jnp.arange(width) // HEAD_DIM
    return jnp.where(h[:, None] == h[None, :], 1.0 / HEAD_DIM, 0.0).astype(BF16)


def _rope_tables(seq):
    rows = seq // GRID_W
    row_id, col_id = jnp.meshgrid(jnp.arange(rows), jnp.arange(GRID_W), indexing='ij')
    row_id = row_id.reshape(-1).astype(F32)
    col_id = col_id.reshape(-1).astype(F32)
    inv_freq = ROPE_THETA ** (-jnp.arange(ROPE_AXIS_FREQS, dtype=F32) / ROPE_AXIS_FREQS)
    ang = jnp.concatenate([row_id[:, None] * inv_freq, col_id[:, None] * inv_freq], axis=-1)
    cos, sin = jnp.cos(ang), jnp.sin(ang)
    reps = LANES // HEAD_DIM
    return jnp.tile(jnp.concatenate([cos, cos], axis=1), (1, reps)), jnp.tile(jnp.concatenate([-sin, sin], axis=1), (1, reps))


def _prep_layer(p):
    perm = _q_lane_perm()
    w_in = p['w_in']
    q0 = 2 * D_LRU
    w_in = jnp.concatenate([w_in[:, :q0], w_in[:, q0:q0 + D_ATT][:, perm], w_in[:, q0 + D_ATT:]], axis=1)
    w_out = p['w_out']
    row = lambda v: v.reshape(1, -1)
    rw_t = p['router_w'].T
    rwh = rw_t.astype(BF16)
    return dict(
        w_in=w_in.astype(BF16),
        gq=jnp.tile(p['q_norm_g'], ATT_HEADS).reshape(1, D_ATT),
        gk=jnp.tile(p['k_norm_g'], KV_HEADS).reshape(1, KV_DIM),
        conv_w=p['conv_w'], conv_b=row(p['conv_b']),
        w_gates=jnp.stack([jnp.concatenate([_block_diag(p['lru_wa'][d]), _block_diag(p['lru_wx'][d])], axis=1)
                           for d in range(2)]).astype(BF16),
        ba=p['lru_ba'].reshape(2, 1, D_LRU), bx=p['lru_bx'].reshape(2, 1, D_LRU),
        lam=p['lru_lambda'].reshape(2, 1, D_LRU),
        gl=row(p['gn_lru_g']), ga=row(p['gn_att_g'][perm]),
        wo1=w_out[:D_LRU].astype(BF16), wo2=w_out[D_LRU:][perm].astype(BF16),
        l1g=row(p['ln1_g']), l1b=row(p['ln1_b']),
        wq=p['xa_wq'].astype(BF16), wkv=p['xa_wkv'].astype(BF16), wxo=p['xa_wo'].astype(BF16),
        l2g=row(p['ln2_g']), l2b=row(p['ln2_b']),
        rwh=rwh, rwl=(rw_t - rwh.astype(F32)).astype(BF16), rb=p['router_b'].reshape(N_EXPERTS, 1),
        wg=p['w_gate'].astype(BF16), wu=p['w_up'].astype(BF16), wd=p['w_down'].astype(BF16),
        sg=p['sh_gate'].astype(BF16), su=p['sh_up'].astype(BF16), sd=p['sh_down'].astype(BF16),
        l3g=row(p['ln3_g']), l3b=row(p['ln3_b']),
    )


def _moe(x2, x2w, lp, alpha, tri):
    n = x2.shape[0]
    eidx, gate, rank, counts, base = _router(x2, lp['rwh'], lp['rwl'], lp['rb'], tri,
                                             lp['sg'], lp['su'], lp['sd'], alpha)
    r = MOE_ROWS
    n_blocks = (n * TOP_K) // r + N_EXPERTS
    counts = counts.reshape(N_EXPERTS)
    pad_counts = ((counts + r - 1) // r) * r
    pad_end = jnp.cumsum(pad_counts)
    pad_start = pad_end - pad_counts
    start_of = jnp.sum(jnp.where(eidx[:, :, None] == jnp.arange(N_EXPERTS), pad_start, 0), axis=-1)
    dest = start_of + rank
    block_e = jnp.minimum(jnp.sum(pad_end[None, :] // r <= jnp.arange(n_blocks)[:, None], axis=1),
                          N_EXPERTS - 1).astype(jnp.int32)
    nvalid = (pad_end[-1] // r).astype(jnp.int32).reshape(1)
    xs = _dispatch((pad_start + counts).astype(jnp.int32), (pad_counts - counts).astype(jnp.int32), nvalid,
                   dest, x2w, n_blocks)
    ys = _experts(block_e, nvalid, xs, lp['wg'], lp['wu'], lp['wd'])
    return base, dest, ys, gate.T


def _layer(x, mem, lp, alpha, consts, first):
    bsz, seq, _ = x.shape
    n = bsz * seq
    xf = x.reshape(n, D_MODEL)
    cos_t, sin_t = _rope_tables(seq)
    g, xbr, q, k, v = _in_proj(xf, seq, consts['ln_g'], consts['ln_b'], lp['w_in'], lp['gq'], lp['gk'],
                               cos_t, sin_t, consts['avgq'], consts['avgk'])
    h = _lru(xbr.reshape(bsz, seq, D_LRU), lp['conv_w'], lp['conv_b'], lp['w_gates'], lp['ba'], lp['bx'], lp['lam'])
    y_att = _attn(q.reshape(bsz, seq, D_ATT), k.reshape(bsz, seq, KV_DIM), v.reshape(bsz, seq, KV_DIM))
    kmem, vmem = _kv_mem(mem.reshape(bsz * N_MEM, D_MODEL), lp['wkv'])
    x2, x2w = _post(xf, g, h.reshape(2, n, D_LRU), y_att.reshape(n, D_ATT), kmem, vmem, seq, alpha,
                    consts['ln_g'], consts['ln_b'], lp['gl'], lp['ga'], lp['wo1'], lp['wo2'], lp['l1g'], lp['l1b'],
                    lp['wq'], lp['wxo'], lp['l2g'], lp['l2b'])
    base, dest, ys, gate_t = _moe(x2, x2w, lp, alpha, consts['tri'])
    out = _combine(dest, base, gate_t, lp['l3g'], lp['l3b'], ys)
    return out.reshape(bsz, seq, D_MODEL)


def kernel(x_prompt, x_sample, mem_prompt, mem_sample, ln_in_g, ln_in_b, w_in, conv_w, conv_b, lru_wa, lru_ba, lru_wx, lru_bx, lru_lambda, q_norm_g, k_norm_g, gn_lru_g, gn_att_g, w_out, ln1_g, ln1_b, xa_wq, xa_wkv, xa_wo, ln2_g, ln2_b, router_w, router_b, w_gate, w_up, w_down, sh_gate, sh_up, sh_down, ln3_g, ln3_b):
    depth = w_in.shape[0]
    assert depth == 1, "the fused in_proj/post kernels assume the input LayerNorm feeds a single layer"
    alpha = (2.0 * depth) ** 0.25
    stacked = dict(w_in=w_in, conv_w=conv_w, conv_b=conv_b, lru_wa=lru_wa, lru_ba=lru_ba, lru_wx=lru_wx,
                   lru_bx=lru_bx, lru_lambda=lru_lambda, q_norm_g=q_norm_g, k_norm_g=k_norm_g, gn_lru_g=gn_lru_g,
                   gn_att_g=gn_att_g, w_out=w_out, ln1_g=ln1_g, ln1_b=ln1_b, xa_wq=xa_wq, xa_wkv=xa_wkv,
                   xa_wo=xa_wo, ln2_g=ln2_g, ln2_b=ln2_b, router_w=router_w, router_b=router_b, w_gate=w_gate,
                   w_up=w_up, w_down=w_down, sh_gate=sh_gate, sh_up=sh_up, sh_down=sh_down, ln3_g=ln3_g, ln3_b=ln3_b)
    lp = _prep_layer({name: val[0] for name, val in stacked.items()})
    tri = (jnp.arange(TOK_TILE)[:, None] < jnp.arange(TOK_TILE)[None, :]).astype(BF16)
    consts = dict(ln_g=ln_in_g.reshape(1, D_MODEL), ln_b=ln_in_b.reshape(1, D_MODEL),
                  avgq=_head_avg(D_ATT), avgk=_head_avg(KV_DIM), tri=tri)
    y_prompt = _layer(x_prompt, mem_prompt, lp, alpha, consts, True)
    y_sample = _layer(x_sample, mem_sample, lp, alpha, consts, False)
    return (y_prompt, y_sample)
```

```python
import functools

import jax
import jax.numpy as jnp
from jax import lax
from jax.experimental import pallas as pl
from jax.experimental.pallas import tpu as pltpu
from jax.experimental.pallas import tpu_sc as plsc

F32 = jnp.float32
BF16 = jnp.bfloat16
U32 = jnp.uint32

D_MODEL = 1024
HEAD_DIM = 64
D_LRU = 512
LRU_BLOCKS = 8
LRU_C = 8.0
CONV_W = 4
CONV_LEFT = 2
ATT_HEADS = 8
KV_HEADS = 2
Q_GROUP = ATT_HEADS // KV_HEADS
D_ATT = ATT_HEADS * HEAD_DIM
KV_DIM = KV_HEADS * HEAD_DIM
D_IN = 2 * D_LRU + D_ATT + 2 * KV_DIM
GRID_W = 64
ROPE_THETA = 10000.0
ROPE_AXIS_FREQS = HEAD_DIM // 4
N_MEM = 256
X_HEADS = 4
X_HEAD_DIM = D_MODEL // X_HEADS
N_EXPERTS = 64
TOP_K = 8
N_GROUPS = 8
GROUP_SIZE = N_EXPERTS // N_GROUPS
TOPK_GROUPS = 4
D_EXPERT = 256
PACKED = D_MODEL // 2
ROUTED_SCALE = 2.5
LN_EPS = 1e-5
RMS_EPS = 1e-6

LANES = 128
SUBLANES = 8
TOK_TILE = 512
LRU_TILE = 512
ATT_TQ = 256
ATT_TK = 2048
MOE_ROWS = 512
ROW_CHUNKS = PACKED // LANES
SC_CORES = 2
SC_SUBCORES = 16
SC_WINDOW = 128
VMEM_LIMIT = 48 * 1024 * 1024


def _cparams(*sem):
    return pltpu.CompilerParams(dimension_semantics=sem, vmem_limit_bytes=VMEM_LIMIT)


def _full(shape):
    return pl.BlockSpec(shape, lambda *_: (0,) * len(shape))


def _layer_norm(x, g, b):
    mu = jnp.mean(x, axis=-1, keepdims=True)
    xc = x - mu
    var = jnp.mean(xc * xc, axis=-1, keepdims=True)
    return xc * lax.rsqrt(var + LN_EPS) * g + b


def _row_rms(x, g):
    return x * lax.rsqrt(jnp.mean(x * x, axis=-1, keepdims=True) + RMS_EPS) * g


def _dot(a, b):
    return jnp.dot(a, b, preferred_element_type=F32)


def _dot_nt(a, b):
    return lax.dot_general(a, b, (((1,), (1,)), ((), ())), preferred_element_type=F32)


def _pack_halves(x):
    w = x.shape[1] // 2
    lo = lax.bitcast_convert_type(x[:, :w].astype(BF16).astype(F32), U32)
    hi = lax.bitcast_convert_type(x[:, w:].astype(BF16).astype(F32), U32)
    return (lo >> 16) | (hi & jnp.uint32(0xFFFF0000))


def _unpack_halves(w):
    lo = lax.bitcast_convert_type(w << 16, F32)
    hi = lax.bitcast_convert_type(w & jnp.uint32(0xFFFF0000), F32)
    return lo, hi


def _store_chunks(ref, w):
    for j in range(ROW_CHUNKS):
        ref[j] = w[:, j * LANES:(j + 1) * LANES]


def _load_chunks(ref):
    return jnp.concatenate([ref[j] for j in range(ROW_CHUNKS)], axis=1)


def _split_bf16(x):
    hi = x.astype(BF16)
    lo = (x - hi.astype(F32)).astype(BF16)
    return hi, lo


def _head_mean_square(z, avg_ref):
    hi, lo = _split_bf16(z * z)
    avg = avg_ref[...]
    return _dot(hi, avg) + _dot(lo, avg)


def _rope(x, cos, sin_signed):
    lane = lax.broadcasted_iota(jnp.int32, (1, LANES), 1)
    low_half = (lane % HEAD_DIM) < (HEAD_DIM // 2)
    outs = []
    for c in range(x.shape[1] // LANES):
        xc = x[:, c * LANES:(c + 1) * LANES]
        partner = jnp.where(low_half, pltpu.roll(xc, LANES - HEAD_DIM // 2, 1), pltpu.roll(xc, HEAD_DIM // 2, 1))
        outs.append(xc * cos + partner * sin_signed)
    return outs[0] if len(outs) == 1 else jnp.concatenate(outs, axis=1)


def _in_proj_kernel(x_ref, lg_ref, lb_ref, w_ref, gq_ref, gk_ref, cos_ref, sin_ref, avgq_ref, avgk_ref,
                    g_out, xbr_out, q_out, k_out, v_out):
    xn = _layer_norm(x_ref[...], lg_ref[...], lb_ref[...])
    z = _dot(xn.astype(BF16), w_ref[...])
    g_out[...] = z[:, :D_LRU]
    xbr_out[...] = z[:, D_LRU:2 * D_LRU]
    zq = z[:, 2 * D_LRU:2 * D_LRU + D_ATT]
    zk = z[:, 2 * D_LRU + D_ATT:2 * D_LRU + D_ATT + KV_DIM]
    v_out[...] = z[:, 2 * D_LRU + D_ATT + KV_DIM:].astype(BF16)
    cos = cos_ref[...]
    sin = sin_ref[...]
    q = zq * lax.rsqrt(_head_mean_square(zq, avgq_ref) + RMS_EPS) * gq_ref[...]
    k = zk * lax.rsqrt(_head_mean_square(zk, avgk_ref) + RMS_EPS) * gk_ref[...]
    q_out[...] = (_rope(q, cos, sin) * (HEAD_DIM ** -0.5)).astype(BF16)
    k_out[...] = _rope(k, cos, sin).astype(BF16)


def _in_proj(x, seq, ln_g, ln_b, w_in_b, gq, gk, cos_t, sin_t, avgq, avgk):
    n = x.shape[0]
    t = TOK_TILE
    nseq = seq // t
    tok = lambda w: pl.BlockSpec((t, w), lambda i: (i, 0))
    pos = pl.BlockSpec((t, LANES), lambda i: (i % nseq, 0))
    return pl.pallas_call(
        _in_proj_kernel,
        grid=(n // t,),
        in_specs=[tok(D_MODEL), _full((1, D_MODEL)), _full((1, D_MODEL)), _full((D_MODEL, D_IN)),
                  _full((1, D_ATT)), _full((1, KV_DIM)), pos, pos, _full((D_ATT, D_ATT)), _full((KV_DIM, KV_DIM))],
        out_specs=[tok(D_LRU), tok(D_LRU), tok(D_ATT), tok(KV_DIM), tok(KV_DIM)],
        out_shape=[jax.ShapeDtypeStruct((n, D_LRU), F32), jax.ShapeDtypeStruct((n, D_LRU), F32),
                   jax.ShapeDtypeStruct((n, D_ATT), BF16), jax.ShapeDtypeStruct((n, KV_DIM), BF16),
                   jax.ShapeDtypeStruct((n, KV_DIM), BF16)],
        compiler_params=_cparams("parallel"),
        name="in_proj",
    )(x, ln_g, ln_b, w_in_b, gq, gk, cos_t, sin_t, avgq, avgk)


def _softplus(x):
    return jnp.maximum(x, 0.0) + jnp.log1p(jnp.exp(-jnp.abs(x)))


def _lru_kernel(xp_ref, x_ref, xn_ref, cw_ref, cb_ref, w_ref, ba_ref, bx_ref, lam_ref, h_ref,
                a_s, b_s, c_s, *, tb, nt):
    d = pl.program_id(1)
    i = pl.program_id(2)
    tbi = jnp.where(d == 0, i, nt - 1 - i)
    x = x_ref[...]
    prev = jnp.where(tbi > 0, xp_ref[...], 0.0)
    nxt = jnp.where(tbi < nt - 1, xn_ref[...], 0.0)
    xe = jnp.concatenate([prev, x, nxt], axis=0)
    cw = cw_ref[...]
    o = SUBLANES - CONV_LEFT
    xc = xe[o:o + tb] * cw[0:1]
    for k in range(1, CONV_W):
        xc = xc + xe[o + k:o + k + tb] * cw[k:k + 1]
    xc = xc + cb_ref[...]
    rg = _dot(xc.astype(BF16), w_ref[...])
    r = jax.nn.sigmoid(rg[:, :D_LRU] + ba_ref[...])
    ig = jax.nn.sigmoid(rg[:, D_LRU:] + bx_ref[...])
    log_a = -LRU_C * r * _softplus(-lam_ref[...])
    a = jnp.exp(log_a)
    b = jnp.sqrt(-jnp.tanh(log_a) * (a * a + 1.0)) * (ig * xc)

    row = lax.broadcasted_iota(jnp.int32, (tb, D_LRU), 0) % SUBLANES

    @pl.when(i == 0)
    def _():
        c_s[...] = jnp.zeros_like(c_s)

    ngroups = tb // SUBLANES

    def scan(reverse):
        aa, bb = a, b
        for k in (1, 2, 4):
            if reverse:
                valid = row < SUBLANES - k
                shift = tb - k
            else:
                valid = row >= k
                shift = k
            a_sh = jnp.where(valid, pltpu.roll(aa, shift, 0), 1.0)
            b_sh = jnp.where(valid, pltpu.roll(bb, shift, 0), 0.0)
            bb = aa * b_sh + bb
            aa = aa * a_sh
        a_s[...] = aa
        b_s[...] = bb

        def body(gi, c):
            g = (ngroups - 1 - gi) if reverse else gi
            sl = pl.ds(pl.multiple_of(g * SUBLANES, SUBLANES), SUBLANES)
            h = a_s[sl, :] * c + b_s[sl, :]
            h_ref[sl, :] = h
            edge = h[0:1, :] if reverse else h[SUBLANES - 1:SUBLANES, :]
            return jnp.broadcast_to(edge, (SUBLANES, D_LRU))

        c_s[...] = lax.fori_loop(0, ngroups, body, c_s[...], unroll=8)

    @pl.when(d == 0)
    def _():
        scan(False)

    @pl.when(d == 1)
    def _():
        scan(True)


def _lru(xbr, conv_w, conv_b, w_gates, ba, bx, lam):
    bsz, seq, _ = xbr.shape
    tb = LRU_TILE
    nt = seq // tb
    r8 = tb // SUBLANES
    nb8 = seq // SUBLANES

    def tblk(d, i):
        return jnp.where(d == 0, i, nt - 1 - i)

    cur = pl.BlockSpec((None, tb, D_LRU), lambda b, d, i: (b, tblk(d, i), 0))
    prv = pl.BlockSpec((None, SUBLANES, D_LRU), lambda b, d, i: (b, jnp.maximum(tblk(d, i) * r8 - 1, 0), 0))
    nxt = pl.BlockSpec((None, SUBLANES, D_LRU), lambda b, d, i: (b, jnp.minimum((tblk(d, i) + 1) * r8, nb8 - 1), 0))
    per_dir = lambda shape: pl.BlockSpec((None,) + shape, lambda b, d, i: (d,) + (0,) * len(shape))
    return pl.pallas_call(
        functools.partial(_lru_kernel, tb=tb, nt=nt),
        grid=(bsz, 2, nt),
        in_specs=[prv, cur, nxt, _full((CONV_W, D_LRU)), _full((1, D_LRU)),
                  per_dir((D_LRU, 2 * D_LRU)), per_dir((1, D_LRU)), per_dir((1, D_LRU)), per_dir((1, D_LRU))],
        out_specs=pl.BlockSpec((None, None, tb, D_LRU), lambda b, d, i: (d, b, tblk(d, i), 0)),
        out_shape=jax.ShapeDtypeStruct((2, bsz, seq, D_LRU), F32),
        scratch_shapes=[pltpu.VMEM((tb, D_LRU), F32), pltpu.VMEM((tb, D_LRU), F32),
                        pltpu.VMEM((SUBLANES, D_LRU), F32)],
        compiler_params=_cparams("parallel", "arbitrary", "arbitrary"),
        name="lru",
    )(xbr, xbr, xbr, conv_w, conv_b, w_gates, ba, bx, lam)


def _attn_kernel(q_ref, k_ref, v_ref, o_ref, qs_s, m_s, l_s, acc_s, *, tq, tk, seq):
    lane = lax.broadcasted_iota(jnp.int32, (1, LANES), 1)
    low = lane < HEAD_DIM
    nchunk = D_ATT // LANES
    for c in range(nchunk):
        qc = q_ref[:, c * LANES:(c + 1) * LANES]
        zero = jnp.zeros_like(qc)
        qs_s[c, :tq, :] = jnp.where(low, qc, zero)
        qs_s[c, tq:, :] = jnp.where(low, zero, qc)
    m_s[...] = jnp.full(m_s.shape, -jnp.inf, F32)
    l_s[...] = jnp.zeros(l_s.shape, F32)
    acc_s[...] = jnp.zeros(acc_s.shape, F32)

    def body(j, carry):
        sl = pl.ds(pl.multiple_of(j * tk, tk), tk)
        kj = k_ref[sl, :]
        vj = v_ref[sl, :]
        for c in range(nchunk):
            s = _dot_nt(qs_s[c], kj)
            m_prev = m_s[c]
            m_next = jnp.maximum(m_prev, jnp.max(s, axis=1, keepdims=True))
            alpha = jnp.exp(m_prev - m_next)
            ps = [jnp.exp(s[:, t * LANES:(t + 1) * LANES] - m_next) for t in range(tk // LANES)]
            part = ps[0]
            for pt in ps[1:]:
                part = part + pt
            l_s[c] = alpha * l_s[c] + part
            p = jnp.concatenate(ps, axis=1).astype(BF16)
            acc_s[c] = alpha * acc_s[c] + _dot(p, vj)
            m_s[c] = m_next
        return carry

    lax.fori_loop(0, seq // tk, body, 0)
    for c in range(nchunk):
        o = acc_s[c] / jnp.sum(l_s[c], axis=1, keepdims=True)
        o_ref[:, c * LANES:(c + 1) * LANES] = jnp.where(low, o[:tq], o[tq:])


def _attn(q, k, v):
    bsz, seq, _ = q.shape
    tq, tk = ATT_TQ, ATT_TK
    kv = pl.BlockSpec((None, seq, KV_DIM), lambda b, i: (b, 0, 0))
    qo = pl.BlockSpec((None, tq, D_ATT), lambda b, i: (b, i, 0))
    stat = pltpu.VMEM((D_ATT // LANES, 2 * tq, LANES), F32)
    return pl.pallas_call(
        functools.partial(_attn_kernel, tq=tq, tk=tk, seq=seq),
        grid=(bsz, seq // tq),
        in_specs=[qo, kv, kv],
        out_specs=qo,
        out_shape=jax.ShapeDtypeStruct((bsz, seq, D_ATT), F32),
        scratch_shapes=[pltpu.VMEM((D_ATT // LANES, 2 * tq, LANES), BF16), stat, stat, stat],
        compiler_params=_cparams("parallel", "parallel"),
        name="attn",
    )(q, k, v)


def _kv_mem_kernel(m_ref, w_ref, k_out, v_out):
    kv = _dot(m_ref[...].astype(BF16), w_ref[...])
    k_out[...] = kv[:, :D_MODEL].astype(BF16)
    v_out[...] = kv[:, D_MODEL:].astype(BF16)


def _kv_mem(mem, wkv_b):
    rows = mem.shape[0]
    blk = pl.BlockSpec((N_MEM, D_MODEL), lambda i: (i, 0))
    return pl.pallas_call(
        _kv_mem_kernel,
        grid=(rows // N_MEM,),
        in_specs=[blk, _full((D_MODEL, 2 * D_MODEL))],
        out_specs=[blk, blk],
        out_shape=[jax.ShapeDtypeStruct((rows, D_MODEL), BF16)] * 2,
        compiler_params=_cparams("parallel"),
        name="kv_mem",
    )(mem, wkv_b)


def _post_kernel(x_ref, g_ref, hf_ref, hb_ref, ya_ref, km_ref, vm_ref,
                 lg_ref, lb_ref, gl_ref, ga_ref, wo1_ref, wo2_ref, l1g_ref, l1b_ref,
                 wq_ref, wxo_ref, l2g_ref, l2b_ref, x2_out, x2w_out, *, alpha):
    x0 = _layer_norm(x_ref[...], lg_ref[...], lb_ref[...])
    y_lru = (hf_ref[...] + hb_ref[...]) * jax.nn.gelu(g_ref[...], approximate=True)
    y_lru = _row_rms(y_lru, gl_ref[...])
    y_att = _row_rms(ya_ref[...], ga_ref[...])
    y = _dot(y_lru.astype(BF16), wo1_ref[...]) + _dot(y_att.astype(BF16), wo2_ref[...])
    x1 = _layer_norm(alpha * x0 + y, l1g_ref[...], l1b_ref[...])
    xq = _dot(x1.astype(BF16), wq_ref[...]).astype(BF16)
    heads = []
    for h in range(X_HEADS):
        sl = slice(h * X_HEAD_DIM, (h + 1) * X_HEAD_DIM)
        s = _dot_nt(xq[:, sl], km_ref[:, sl]) * (X_HEAD_DIM ** -0.5)
        e = jnp.exp(s - jnp.max(s, axis=1, keepdims=True))
        o = _dot(e.astype(BF16), vm_ref[:, sl]) / jnp.sum(e, axis=1, keepdims=True)
        heads.append(o.astype(BF16))
    o = jnp.concatenate(heads, axis=1)
    x2 = _layer_norm(alpha * x1 + _dot(o, wxo_ref[...]), l2g_ref[...], l2b_ref[...])
    x2_out[...] = x2
    _store_chunks(x2w_out, _pack_halves(x2))


def _post(x, g, h, y_att, kmem, vmem, seq, alpha, ln_g, ln_b, gl, ga, wo1, wo2, l1g, l1b, wq, wxo, l2g, l2b):
    n = x.shape[0]
    t = TOK_TILE
    nseq = seq // t
    tok = lambda w: pl.BlockSpec((t, w), lambda i: (i, 0))
    hdir = lambda d: pl.BlockSpec((None, t, D_LRU), lambda i: (d, i, 0))
    mem = pl.BlockSpec((N_MEM, D_MODEL), lambda i: (i // nseq, 0))
    vec = _full((1, D_MODEL))
    half = _full((1, D_LRU))
    sq = _full((D_MODEL, D_MODEL))
    return pl.pallas_call(
        functools.partial(_post_kernel, alpha=alpha),
        grid=(n // t,),
        in_specs=[tok(D_MODEL), tok(D_LRU), hdir(0), hdir(1), tok(D_ATT), mem, mem,
                  vec, vec, half, half, _full((D_LRU, D_MODEL)), _full((D_ATT, D_MODEL)), vec, vec,
                  sq, sq, vec, vec],
        out_specs=[tok(D_MODEL), pl.BlockSpec((ROW_CHUNKS, t, LANES), lambda i: (0, i, 0))],
        out_shape=[jax.ShapeDtypeStruct((n, D_MODEL), F32), jax.ShapeDtypeStruct((ROW_CHUNKS, n, LANES), U32)],
        compiler_params=_cparams("parallel"),
        name="post",
    )(x, g, h, h, y_att, kmem, vmem, ln_g, ln_b, gl, ga, wo1, wo2, l1g, l1b, wq, wxo, l2g, l2b)


def _first_max(v, n):
    rows = lax.broadcasted_iota(jnp.int32, v.shape, 0)
    m = jnp.max(v, axis=0, keepdims=True)
    idx = jnp.min(jnp.where(v == m, rows, n), axis=0, keepdims=True)
    return m, idx, rows == idx


def _router_kernel(x_ref, rwh_ref, rwl_ref, rb_ref, tri_ref, sg_ref, su_ref, sd_ref,
                   eidx_out, gate_out, rank_out, cnt_out, base_out, cnt_s, *, alpha):
    step = pl.program_id(0)

    @pl.when(step == 0)
    def _():
        cnt_s[...] = jnp.zeros_like(cnt_s)

    x = x_ref[...]
    xh, xl = _split_bf16(x)
    logits = _dot_nt(rwh_ref[...], xh) + _dot_nt(rwh_ref[...], xl) + _dot_nt(rwl_ref[...], xh)
    scores = jax.nn.sigmoid(logits)
    biased = scores + rb_ref[...]
    t = x.shape[0]
    neg = -jnp.inf

    gscore = []
    for g in range(N_GROUPS):
        vg = biased[g * GROUP_SIZE:(g + 1) * GROUP_SIZE, :]
        m1, _, hit = _first_max(vg, GROUP_SIZE)
        m2 = jnp.max(jnp.where(hit, neg, vg), axis=0, keepdims=True)
        gscore.append(m1 + m2)
    gs = jnp.concatenate(gscore, axis=0)
    gsel = jnp.zeros(gs.shape, jnp.bool_)
    for _ in range(TOPK_GROUPS):
        _, _, hit = _first_max(gs, N_GROUPS)
        gsel = jnp.logical_or(gsel, hit)
        gs = jnp.where(hit, neg, gs)
    emask = jnp.concatenate(
        [jnp.broadcast_to(gsel[g:g + 1, :], (GROUP_SIZE, t)) for g in range(N_GROUPS)], axis=0)
    masked = jnp.where(emask, biased, neg)

    idxs, gates, hits = [], [], []
    sel = jnp.zeros(masked.shape, jnp.bool_)
    for _ in range(TOP_K):
        _, idx, hit = _first_max(masked, N_EXPERTS)
        idxs.append(idx)
        hits.append(hit)
        gates.append(jnp.sum(jnp.where(hit, scores, 0.0), axis=0, keepdims=True))
        sel = jnp.logical_or(sel, hit)
        masked = jnp.where(hit, neg, masked)
    gsum = gates[0]
    for gk in gates[1:]:
        gsum = gsum + gk
    gate = jnp.concatenate(gates, axis=0) / gsum * ROUTED_SCALE

    chosen = jnp.where(sel, 1.0, 0.0)
    before = _dot(chosen.astype(BF16), tri_ref[...]) + cnt_s[...]
    ranks = [jnp.sum(jnp.where(hit, before, 0.0), axis=0, keepdims=True) for hit in hits]
    cnt = cnt_s[...] + jnp.sum(chosen, axis=1, keepdims=True)
    cnt_s[...] = cnt

    eidx_out[...] = jnp.concatenate(idxs, axis=0)
    gate_out[...] = gate
    rank_out[...] = jnp.concatenate(ranks, axis=0).astype(jnp.int32)
    cnt_out[...] = cnt.astype(jnp.int32)

    hmid = jax.nn.silu(_dot(xh, sg_ref[...])) * _dot(xh, su_ref[...])
    base_out[...] = alpha * x + _dot(hmid.astype(BF16), sd_ref[...])


def _router(x2, rwh, rwl, rb, tri, sg, su, sd, alpha):
    n = x2.shape[0]
    t = TOK_TILE
    tok = pl.BlockSpec((t, D_MODEL), lambda i: (i, 0))
    kt = pl.BlockSpec((TOP_K, t), lambda i: (0, i))
    return pl.pallas_call(
        functools.partial(_router_kernel, alpha=alpha),
        grid=(n // t,),
        in_specs=[tok, _full((N_EXPERTS, D_MODEL)), _full((N_EXPERTS, D_MODEL)), _full((N_EXPERTS, 1)),
                  _full((t, t)), _full((D_MODEL, D_EXPERT)), _full((D_MODEL, D_EXPERT)), _full((D_EXPERT, D_MODEL))],
        out_specs=[kt, kt, kt, _full((N_EXPERTS, 1)), tok],
        out_shape=[jax.ShapeDtypeStruct((TOP_K, n), jnp.int32), jax.ShapeDtypeStruct((TOP_K, n), F32),
                   jax.ShapeDtypeStruct((TOP_K, n), jnp.int32), jax.ShapeDtypeStruct((N_EXPERTS, 1), jnp.int32),
                   jax.ShapeDtypeStruct((n, D_MODEL), F32)],
        scratch_shapes=[pltpu.VMEM((N_EXPERTS, 1), F32)],
        compiler_params=_cparams("arbitrary"),
        name="router",
    )(x2, rwh, rwl, rb, tri, sg, su, sd)


def _sc_mesh():
    return plsc.VectorSubcoreMesh(core_axis_name="core", subcore_axis_name="subcore")


def _sc_worker_base(per_worker):
    return (lax.axis_index("subcore") * SC_CORES + lax.axis_index("core")) * per_worker


def _sc_dispatch(x_sub, sidx, rows_out):
    nsub = x_sub.shape[0]
    win = SC_WINDOW
    per_worker = nsub // (SC_CORES * SC_SUBCORES)
    assert per_worker * SC_CORES * SC_SUBCORES == nsub and per_worker % win == 0

    def body(x_hbm, idx_hbm, out_hbm, idx_v, rows_v, sem):
        base = _sc_worker_base(per_worker)

        @pl.loop(0, per_worker // win)
        def _(ci):
            off = base + ci * win
            pltpu.sync_copy(idx_hbm.at[:, pl.ds(off, win)], idx_v)
            pltpu.sync_copy(x_hbm.at[pl.ds(off, win)], rows_v)
            copies = [pltpu.async_copy(rows_v, out_hbm.at[idx_v.at[k]], sem) for k in range(TOP_K)]
            for cp in copies:
                cp.wait()

    return pl.kernel(
        body, mesh=_sc_mesh(), out_type=jax.ShapeDtypeStruct((rows_out, LANES), U32),
        scratch_types=[pltpu.VMEM((TOP_K, win), jnp.int32), pltpu.VMEM((win, LANES), U32), pltpu.SemaphoreType.DMA],
    )(x_sub, sidx)


def _sc_gather(y_sub, sidx):
    nsub = sidx.shape[1]
    win = SC_WINDOW
    half = TOP_K // 2
    per_worker = nsub // (SC_CORES * SC_SUBCORES)
    assert per_worker * SC_CORES * SC_SUBCORES == nsub and per_worker % win == 0

    def body(y_hbm, idx_hbm, out_hbm, idx_v, rows_v, gsem, osem):
        base = _sc_worker_base(per_worker)

        @pl.loop(0, per_worker // win)
        def _(ci):
            off = base + ci * win
            pltpu.sync_copy(idx_hbm.at[:, pl.ds(off, win)], idx_v)
            for h in range(TOP_K // half):
                gathers = [pltpu.async_copy(y_hbm.at[idx_v.at[h * half + k]], rows_v.at[k], gsem) for k in range(half)]
                for cp in gathers:
                    cp.wait()
                stores = [pltpu.async_copy(rows_v.at[k], out_hbm.at[h * half + k, pl.ds(off, win)], osem)
                          for k in range(half)]
                for cp in stores:
                    cp.wait()

    return pl.kernel(
        body, mesh=_sc_mesh(), out_type=jax.ShapeDtypeStruct((TOP_K, nsub, LANES), U32),
        scratch_types=[pltpu.VMEM((TOP_K, win), jnp.int32), pltpu.VMEM((half, win, LANES), U32),
                       pltpu.SemaphoreType.DMA, pltpu.SemaphoreType.DMA],
    )(y_sub, sidx)


def _experts_kernel(be_ref, nv_ref, xs_ref, wg_ref, wu_ref, wd_ref, y_out):
    @pl.when(pl.program_id(0) < nv_ref[0])
    def _():
        lo, hi = _unpack_halves(_load_chunks(xs_ref))
        lo = lo.astype(BF16)
        hi = hi.astype(BF16)
        gate = _dot(lo, wg_ref[:PACKED, :]) + _dot(hi, wg_ref[PACKED:, :])
        up = _dot(lo, wu_ref[:PACKED, :]) + _dot(hi, wu_ref[PACKED:, :])
        hmid = jax.nn.silu(gate) * up
        _store_chunks(y_out, _pack_halves(_dot(hmid.astype(BF16), wd_ref[...])))

    @pl.when(pl.program_id(0) >= nv_ref[0])
    def _():
        y_out[...] = jnp.zeros_like(y_out)


def _experts(block_e, nvalid, xs, wg, wu, wd):
    rows = xs.shape[1]
    r = MOE_ROWS
    blk = pl.BlockSpec((ROW_CHUNKS, r, LANES), lambda i, be, nv: (0, i, 0))
    return pl.pallas_call(
        _experts_kernel,
        grid_spec=pltpu.PrefetchScalarGridSpec(
            num_scalar_prefetch=2,
            grid=(rows // r,),
            in_specs=[blk,
                      pl.BlockSpec((None, D_MODEL, D_EXPERT), lambda i, be, nv: (be[i], 0, 0)),
                      pl.BlockSpec((None, D_MODEL, D_EXPERT), lambda i, be, nv: (be[i], 0, 0)),
                      pl.BlockSpec((None, D_EXPERT, D_MODEL), lambda i, be, nv: (be[i], 0, 0))],
            out_specs=blk,
        ),
        out_shape=jax.ShapeDtypeStruct((ROW_CHUNKS, rows, LANES), U32),
        compiler_params=_cparams("arbitrary"),
        name="experts",
    )(block_e, nvalid, xs, wg, wu, wd)


def _combine_kernel(base_ref, yg_ref, gate_ref, lg_ref, lb_ref, o_ref):
    gate = gate_ref[...]
    routed = None
    for k in range(TOP_K):
        lo, hi = _unpack_halves(_load_chunks(yg_ref.at[k]))
        yk = jnp.concatenate([lo, hi], axis=1) * gate[:, k:k + 1]
        routed = yk if routed is None else routed + yk
    o_ref[...] = _layer_norm(base_ref[...] + routed, lg_ref[...], lb_ref[...])


def _combine(base, yg, gate_t, lg, lb):
    n = base.shape[0]
    t = TOK_TILE
    tok = pl.BlockSpec((t, D_MODEL), lambda i: (i, 0))
    return pl.pallas_call(
        _combine_kernel,
        grid=(n // t,),
        in_specs=[tok, pl.BlockSpec((TOP_K, ROW_CHUNKS, t, LANES), lambda i: (0, 0, i, 0)),
                  pl.BlockSpec((t, TOP_K), lambda i: (i, 0)), _full((1, D_MODEL)), _full((1, D_MODEL))],
        out_specs=tok,
        out_shape=jax.ShapeDtypeStruct((n, D_MODEL), F32),
        compiler_params=_cparams("parallel"),
        name="combine",
    )(base, yg, gate_t, lg, lb)


def _q_lane_perm():
    order = []
    for c in range(Q_GROUP):
        for g in range(KV_HEADS):
            order.append(g * Q_GROUP + c)
    idx = []
    for h in order:
        idx.extend(range(h * HEAD_DIM, (h + 1) * HEAD_DIM))
    return jnp.asarray(idx, jnp.int32)


def _block_diag(w):
    nb, bi, bo = w.shape
    eye = jnp.eye(nb, dtype=w.dtype)
    return (eye[:, None, :, None] * w[:, :, None, :]).reshape(nb * bi, nb * bo)


def _head_avg(width):
    h = jnp.arange(width) // HEAD_DIM
    return jnp.where(h[:, None] == h[None, :], 1.0 / HEAD_DIM, 0.0).astype(BF16)


def _rope_tables(seq):
    rows = seq // GRID_W
    row_id, col_id = jnp.meshgrid(jnp.arange(rows), jnp.arange(GRID_W), indexing='ij')
    row_id = row_id.reshape(-1).astype(F32)
    col_id = col_id.reshape(-1).astype(F32)
    inv_freq = ROPE_THETA ** (-jnp.arange(ROPE_AXIS_FREQS, dtype=F32) / ROPE_AXIS_FREQS)
    ang = jnp.concatenate([row_id[:, None] * inv_freq, col_id[:, None] * inv_freq], axis=-1)
    cos, sin = jnp.cos(ang), jnp.sin(ang)
    reps = LANES // HEAD_DIM
    return jnp.tile(jnp.concatenate([cos, cos], axis=1), (1, reps)), jnp.tile(jnp.concatenate([-sin, sin], axis=1), (1, reps))


def _prep_layer(p):
    perm = _q_lane_perm()
    w_in = p['w_in']
    q0 = 2 * D_LRU
    w_in = jnp.concatenate([w_in[:, :q0], w_in[:, q0:q0 + D_ATT][:, perm], w_in[:, q0 + D_ATT:]], axis=1)
    w_out = p['w_out']
    row = lambda v: v.reshape(1, -1)
    rw_t = p['router_w'].T
    rwh = rw_t.astype(BF16)
    return dict(
        w_in=w_in.astype(BF16),
        gq=jnp.tile(p['q_norm_g'], ATT_HEADS).reshape(1, D_ATT),
        gk=jnp.tile(p['k_norm_g'], KV_HEADS).reshape(1, KV_DIM),
        conv_w=p['conv_w'], conv_b=row(p['conv_b']),
        w_gates=jnp.stack([jnp.concatenate([_block_diag(p['lru_wa'][d]), _block_diag(p['lru_wx'][d])], axis=1)
                           for d in range(2)]).astype(BF16),
        ba=p['lru_ba'].reshape(2, 1, D_LRU), bx=p['lru_bx'].reshape(2, 1, D_LRU),
        lam=p['lru_lambda'].reshape(2, 1, D_LRU),
        gl=row(p['gn_lru_g']), ga=row(p['gn_att_g'][perm]),
        wo1=w_out[:D_LRU].astype(BF16), wo2=w_out[D_LRU:][perm].astype(BF16),
        l1g=row(p['ln1_g']), l1b=row(p['ln1_b']),
        wq=p['xa_wq'].astype(BF16), wkv=p['xa_wkv'].astype(BF16), wxo=p['xa_wo'].astype(BF16),
        l2g=row(p['ln2_g']), l2b=row(p['ln2_b']),
        rwh=rwh, rwl=(rw_t - rwh.astype(F32)).astype(BF16), rb=p['router_b'].reshape(N_EXPERTS, 1),
        wg=p['w_gate'].astype(BF16), wu=p['w_up'].astype(BF16), wd=p['w_down'].astype(BF16),
        sg=p['sh_gate'].astype(BF16), su=p['sh_up'].astype(BF16), sd=p['sh_down'].astype(BF16),
        l3g=row(p['ln3_g']), l3b=row(p['ln3_b']),
    )


def _moe(x2, x2w, lp, alpha, tri):
    n = x2.shape[0]
    eidx, gate, rank, counts, base = _router(x2, lp['rwh'], lp['rwl'], lp['rb'], tri,
                                             lp['sg'], lp['su'], lp['sd'], alpha)
    r = MOE_ROWS
    n_blocks = (n * TOP_K) // r + N_EXPERTS
    rows = n_blocks * r
    counts = counts.reshape(N_EXPERTS)
    pad_counts = ((counts + r - 1) // r) * r
    pad_end = jnp.cumsum(pad_counts)
    pad_start = pad_end - pad_counts
    start_of = jnp.sum(jnp.where(eidx[:, :, None] == jnp.arange(N_EXPERTS), pad_start, 0), axis=-1)
    dest = start_of + rank
    block_e = jnp.minimum(jnp.sum(pad_end[None, :] // r <= jnp.arange(n_blocks)[:, None], axis=1),
                          N_EXPERTS - 1).astype(jnp.int32)
    nvalid = (pad_end[-1] // r).astype(jnp.int32).reshape(1)
    sidx = (dest[:, None, :] + (jnp.arange(ROW_CHUNKS, dtype=jnp.int32) * rows)[None, :, None]).reshape(TOP_K, ROW_CHUNKS * n)
    xs = _sc_dispatch(x2w.reshape(ROW_CHUNKS * n, LANES), sidx, ROW_CHUNKS * rows)
    ys = _experts(block_e, nvalid, xs.reshape(ROW_CHUNKS, rows, LANES), lp['wg'], lp['wu'], lp['wd'])
    yg = _sc_gather(ys.reshape(ROW_CHUNKS * rows, LANES), sidx)
    return base, yg.reshape(TOP_K, ROW_CHUNKS, n, LANES), gate.T


def _layer(x, mem, lp, alpha, consts, first):
    bsz, seq, _ = x.shape
    n = bsz * seq
    xf = x.reshape(n, D_MODEL)
    cos_t, sin_t = _rope_tables(seq)
    g, xbr, q, k, v = _in_proj(xf, seq, consts['ln_g'], consts['ln_b'], lp['w_in'], lp['gq'], lp['gk'],
                               cos_t, sin_t, consts['avgq'], consts['avgk'])
    h = _lru(xbr.reshape(bsz, seq, D_LRU), lp['conv_w'], lp['conv_b'], lp['w_gates'], lp['ba'], lp['bx'], lp['lam'])
    y_att = _attn(q.reshape(bsz, seq, D_ATT), k.reshape(bsz, seq, KV_DIM), v.reshape(bsz, seq, KV_DIM))
    kmem, vmem = _kv_mem(mem.reshape(bsz * N_MEM, D_MODEL), lp['wkv'])
    x2, x2w = _post(xf, g, h.reshape(2, n, D_LRU), y_att.reshape(n, D_ATT), kmem, vmem, seq, alpha,
                    consts['ln_g'], consts['ln_b'], lp['gl'], lp['ga'], lp['wo1'], lp['wo2'], lp['l1g'], lp['l1b'],
                    lp['wq'], lp['wxo'], lp['l2g'], lp['l2b'])
    base, yg, gate_t = _moe(x2, x2w, lp, alpha, consts['tri'])
    out = _combine(base, yg, gate_t, lp['l3g'], lp['l3b'])
    return out.reshape(bsz, seq, D_MODEL)


def kernel(x_prompt, x_sample, mem_prompt, mem_sample, ln_in_g, ln_in_b, w_in, conv_w, conv_b, lru_wa, lru_ba, lru_wx, lru_bx, lru_lambda, q_norm_g, k_norm_g, gn_lru_g, gn_att_g, w_out, ln1_g, ln1_b, xa_wq, xa_wkv, xa_wo, ln2_g, ln2_b, router_w, router_b, w_gate, w_up, w_down, sh_gate, sh_up, sh_down, ln3_g, ln3_b):
    depth = w_in.shape[0]
    assert depth == 1, "the fused in_proj/post kernels assume the input LayerNorm feeds a single layer"
    alpha = (2.0 * depth) ** 0.25
    stacked = dict(w_in=w_in, conv_w=conv_w, conv_b=conv_b, lru_wa=lru_wa, lru_ba=lru_ba, lru_wx=lru_wx,
                   lru_bx=lru_bx, lru_lambda=lru_lambda, q_norm_g=q_norm_g, k_norm_g=k_norm_g, gn_lru_g=gn_lru_g,
                   gn_att_g=gn_att_g, w_out=w_out, ln1_g=ln1_g, ln1_b=ln1_b, xa_wq=xa_wq, xa_wkv=xa_wkv,
                   xa_wo=xa_wo, ln2_g=ln2_g, ln2_b=ln2_b, router_w=router_w, router_b=router_b, w_gate=w_gate,
                   w_up=w_up, w_down=w_down, sh_gate=sh_gate, sh_up=sh_up, sh_down=sh_down, ln3_g=ln3_g, ln3_b=ln3_b)
    lp = _prep_layer({name: val[0] for name, val in stacked.items()})
    tri = (jnp.arange(TOK_TILE)[:, None] < jnp.arange(TOK_TILE)[None, :]).astype(BF16)
    consts = dict(ln_g=ln_in_g.reshape(1, D_MODEL), ln_b=ln_in_b.reshape(1, D_MODEL),
                  avgq=_head_avg(D_ATT), avgk=_head_avg(KV_DIM), tri=tri)
    y_prompt = _layer(x_prompt, mem_prompt, lp, alpha, consts, True)
    y_sample = _layer(x_sample, mem_sample, lp, alpha, consts, False)
    return (y_prompt, y_sample)
```

```python
import functools

import jax
import jax.numpy as jnp
from jax import lax
from jax.experimental import pallas as pl
from jax.experimental.pallas import tpu as pltpu
from jax.experimental.pallas import tpu_sc as plsc

F32 = jnp.float32
BF16 = jnp.bfloat16
U32 = jnp.uint32

D_MODEL = 1024
HEAD_DIM = 64
D_LRU = 512
LRU_BLOCKS = 8
LRU_C = 8.0
CONV_W = 4
CONV_LEFT = 2
ATT_HEADS = 8
KV_HEADS = 2
Q_GROUP = ATT_HEADS // KV_HEADS
D_ATT = ATT_HEADS * HEAD_DIM
KV_DIM = KV_HEADS * HEAD_DIM
D_IN = 2 * D_LRU + D_ATT + 2 * KV_DIM
GRID_W = 64
ROPE_THETA = 10000.0
ROPE_AXIS_FREQS = HEAD_DIM // 4
N_MEM = 256
X_HEADS = 4
X_HEAD_DIM = D_MODEL // X_HEADS
N_EXPERTS = 64
TOP_K = 8
N_GROUPS = 8
GROUP_SIZE = N_EXPERTS // N_GROUPS
TOPK_GROUPS = 4
D_EXPERT = 256
PACKED = D_MODEL // 2
ROUTED_SCALE = 2.5
LN_EPS = 1e-5
RMS_EPS = 1e-6
LOG2_E = 1.4426950408889634

LANES = 128
SUBLANES = 8
MXU_DIM = 256
TOK_TILE = 512
LRU_TILE = 512
ATT_TQ = 256
ATT_TK = 2048
MOE_ROWS = 512
EXPERT_BLOCKS_PER_STEP = 2
ROW_CHUNKS = PACKED // LANES
SC_CORES = 2
SC_SUBCORES = 16
SC_WINDOW = 128
VMEM_LIMIT = 48 * 1024 * 1024


def _cparams(*sem):
    return pltpu.CompilerParams(dimension_semantics=sem, vmem_limit_bytes=VMEM_LIMIT)


def _full(shape):
    return pl.BlockSpec(shape, lambda *_: (0,) * len(shape))


def _layer_norm(x, g, b):
    mu = jnp.mean(x, axis=-1, keepdims=True)
    xc = x - mu
    var = jnp.mean(xc * xc, axis=-1, keepdims=True)
    return xc * lax.rsqrt(var + LN_EPS) * g + b


def _row_rms(x, g):
    return x * lax.rsqrt(jnp.mean(x * x, axis=-1, keepdims=True) + RMS_EPS) * g


def _dot(a, b):
    return jnp.dot(a, b, preferred_element_type=F32)


def _dot_nt(a, b):
    return lax.dot_general(a, b, (((1,), (1,)), ((), ())), preferred_element_type=F32)


def _pack_halves(x):
    w = x.shape[1] // 2
    lo = lax.bitcast_convert_type(x[:, :w].astype(BF16).astype(F32), U32)
    hi = lax.bitcast_convert_type(x[:, w:].astype(BF16).astype(F32), U32)
    return (lo >> 16) | (hi & jnp.uint32(0xFFFF0000))


def _unpack_halves(w):
    lo = lax.bitcast_convert_type(w << 16, F32)
    hi = lax.bitcast_convert_type(w & jnp.uint32(0xFFFF0000), F32)
    return lo, hi


def _store_chunks(ref, w):
    for j in range(ROW_CHUNKS):
        ref[j] = w[:, j * LANES:(j + 1) * LANES]


def _load_chunks(ref):
    return jnp.concatenate([ref[j] for j in range(ROW_CHUNKS)], axis=1)


def _split_bf16(x):
    hi = x.astype(BF16)
    lo = (x - hi.astype(F32)).astype(BF16)
    return hi, lo


def _head_mean_square(z, avg_ref):
    hi, lo = _split_bf16(z * z)
    avg = avg_ref[...]
    w = avg.shape[0]
    parts = [_dot(hi[:, c:c + w], avg) + _dot(lo[:, c:c + w], avg) for c in range(0, z.shape[1], w)]
    return parts[0] if len(parts) == 1 else jnp.concatenate(parts, axis=1)


def _rope(x, cos, sin_signed):
    lane = lax.broadcasted_iota(jnp.int32, (1, LANES), 1)
    low_half = (lane % HEAD_DIM) < (HEAD_DIM // 2)
    outs = []
    for c in range(x.shape[1] // LANES):
        xc = x[:, c * LANES:(c + 1) * LANES]
        partner = jnp.where(low_half, pltpu.roll(xc, LANES - HEAD_DIM // 2, 1), pltpu.roll(xc, HEAD_DIM // 2, 1))
        outs.append(xc * cos + partner * sin_signed)
    return outs[0] if len(outs) == 1 else jnp.concatenate(outs, axis=1)


def _in_proj_kernel(x_ref, lg_ref, lb_ref, w_ref, gq_ref, gk_ref, cos_ref, sin_ref, avgq_ref, avgk_ref,
                    g_out, xbr_out, q_out, k_out, v_out):
    xn = _layer_norm(x_ref[...], lg_ref[...], lb_ref[...])
    z = _dot(xn.astype(BF16), w_ref[...])
    g_out[...] = z[:, :D_LRU]
    xbr_out[...] = z[:, D_LRU:2 * D_LRU]
    zq = z[:, 2 * D_LRU:2 * D_LRU + D_ATT]
    zk = z[:, 2 * D_LRU + D_ATT:2 * D_LRU + D_ATT + KV_DIM]
    v_out[...] = z[:, 2 * D_LRU + D_ATT + KV_DIM:].astype(BF16)
    cos = cos_ref[...]
    sin = sin_ref[...]
    q = zq * lax.rsqrt(_head_mean_square(zq, avgq_ref) + RMS_EPS) * gq_ref[...]
    k = zk * lax.rsqrt(_head_mean_square(zk, avgk_ref) + RMS_EPS) * gk_ref[...]
    q_out[...] = (_rope(q, cos, sin) * (HEAD_DIM ** -0.5 * LOG2_E)).astype(BF16)
    k_out[...] = _rope(k, cos, sin).astype(BF16)


def _in_proj(x, seq, ln_g, ln_b, w_in_b, gq, gk, cos_t, sin_t, avgq, avgk):
    n = x.shape[0]
    t = TOK_TILE
    nseq = seq // t
    tok = lambda w: pl.BlockSpec((t, w), lambda i: (i, 0))
    pos = pl.BlockSpec((t, LANES), lambda i: (i % nseq, 0))
    return pl.pallas_call(
        _in_proj_kernel,
        grid=(n // t,),
        in_specs=[tok(D_MODEL), _full((1, D_MODEL)), _full((1, D_MODEL)), _full((D_MODEL, D_IN)),
                  _full((1, D_ATT)), _full((1, KV_DIM)), pos, pos, _full(avgq.shape), _full(avgk.shape)],
        out_specs=[tok(D_LRU), tok(D_LRU), tok(D_ATT), tok(KV_DIM), tok(KV_DIM)],
        out_shape=[jax.ShapeDtypeStruct((n, D_LRU), F32), jax.ShapeDtypeStruct((n, D_LRU), F32),
                   jax.ShapeDtypeStruct((n, D_ATT), BF16), jax.ShapeDtypeStruct((n, KV_DIM), BF16),
                   jax.ShapeDtypeStruct((n, KV_DIM), BF16)],
        compiler_params=_cparams("parallel"),
        name="in_proj",
    )(x, ln_g, ln_b, w_in_b, gq, gk, cos_t, sin_t, avgq, avgk)


def _softplus(x):
    return jnp.maximum(x, 0.0) + jnp.log1p(jnp.exp(-jnp.abs(x)))


def _lru_kernel(xp_ref, x_ref, xn_ref, cw_ref, cb_ref, w_ref, ba_ref, bx_ref, lam_ref, h_ref,
                a_s, b_s, c_s, *, tb, nt):
    d = pl.program_id(1)
    i = pl.program_id(2)
    tbi = jnp.where(d == 0, i, nt - 1 - i)
    x = x_ref[...]
    prev = jnp.where(tbi > 0, xp_ref[...], 0.0)
    nxt = jnp.where(tbi < nt - 1, xn_ref[...], 0.0)
    xe = jnp.concatenate([prev, x, nxt], axis=0)
    cw = cw_ref[...]
    o = SUBLANES - CONV_LEFT
    xc = xe[o:o + tb] * cw[0:1]
    for k in range(1, CONV_W):
        xc = xc + xe[o + k:o + k + tb] * cw[k:k + 1]
    xc = xc + cb_ref[...]
    rg = _dot(xc.astype(BF16), w_ref[...])
    r = jax.nn.sigmoid(rg[:, :D_LRU] + ba_ref[...])
    ig = jax.nn.sigmoid(rg[:, D_LRU:] + bx_ref[...])
    log_a = -LRU_C * r * _softplus(-lam_ref[...])
    a = jnp.exp(log_a)
    b = jnp.sqrt(-jnp.tanh(log_a) * (a * a + 1.0)) * (ig * xc)

    ngroups = tb // SUBLANES
    row = lax.broadcasted_iota(jnp.int32, (ngroups, SUBLANES, D_LRU), 1)

    @pl.when(i == 0)
    def _():
        c_s[...] = jnp.zeros_like(c_s)

    def scan(reverse):
        aa = a.reshape(ngroups, SUBLANES, D_LRU)
        bb = b.reshape(ngroups, SUBLANES, D_LRU)
        for k in (1, 2, 4):
            if reverse:
                valid = row < SUBLANES - k
                shift = SUBLANES - k
            else:
                valid = row >= k
                shift = k
            a_sh = jnp.where(valid, pltpu.roll(aa, shift, 1), 1.0)
            b_sh = jnp.where(valid, pltpu.roll(bb, shift, 1), 0.0)
            bb = aa * b_sh + bb
            aa = aa * a_sh
        a_s[...] = aa.reshape(tb, D_LRU)
        b_s[...] = bb.reshape(tb, D_LRU)

        def body(gi, c):
            g = (ngroups - 1 - gi) if reverse else gi
            sl = pl.ds(pl.multiple_of(g * SUBLANES, SUBLANES), SUBLANES)
            h = a_s[sl, :] * c + b_s[sl, :]
            h_ref[sl, :] = h
            edge = h[0:1, :] if reverse else h[SUBLANES - 1:SUBLANES, :]
            return jnp.broadcast_to(edge, (SUBLANES, D_LRU))

        c_s[...] = lax.fori_loop(0, ngroups, body, c_s[...], unroll=8)

    @pl.when(d == 0)
    def _():
        scan(False)

    @pl.when(d == 1)
    def _():
        scan(True)


def _lru(xbr, conv_w, conv_b, w_gates, ba, bx, lam):
    bsz, seq, _ = xbr.shape
    tb = LRU_TILE
    nt = seq // tb
    r8 = tb // SUBLANES
    nb8 = seq // SUBLANES

    def tblk(d, i):
        return jnp.where(d == 0, i, nt - 1 - i)

    cur = pl.BlockSpec((None, tb, D_LRU), lambda b, d, i: (b, tblk(d, i), 0))
    prv = pl.BlockSpec((None, SUBLANES, D_LRU), lambda b, d, i: (b, jnp.maximum(tblk(d, i) * r8 - 1, 0), 0))
    nxt = pl.BlockSpec((None, SUBLANES, D_LRU), lambda b, d, i: (b, jnp.minimum((tblk(d, i) + 1) * r8, nb8 - 1), 0))
    per_dir = lambda shape: pl.BlockSpec((None,) + shape, lambda b, d, i: (d,) + (0,) * len(shape))
    return pl.pallas_call(
        functools.partial(_lru_kernel, tb=tb, nt=nt),
        grid=(bsz, 2, nt),
        in_specs=[prv, cur, nxt, _full((CONV_W, D_LRU)), _full((1, D_LRU)),
                  per_dir((D_LRU, 2 * D_LRU)), per_dir((1, D_LRU)), per_dir((1, D_LRU)), per_dir((1, D_LRU))],
        out_specs=pl.BlockSpec((None, None, tb, D_LRU), lambda b, d, i: (d, b, tblk(d, i), 0)),
        out_shape=jax.ShapeDtypeStruct((2, bsz, seq, D_LRU), F32),
        scratch_shapes=[pltpu.VMEM((tb, D_LRU), F32), pltpu.VMEM((tb, D_LRU), F32),
                        pltpu.VMEM((SUBLANES, D_LRU), F32)],
        compiler_params=_cparams("parallel", "arbitrary", "arbitrary"),
        name="lru",
    )(xbr, xbr, xbr, conv_w, conv_b, w_gates, ba, bx, lam)


def _attn_kernel(q_ref, k_ref, v_ref, o_ref, qs_s, m_s, l_s, acc_s, *, tq, tk, seq):
    lane = lax.broadcasted_iota(jnp.int32, (1, LANES), 1)
    low = lane < HEAD_DIM
    nchunk = D_ATT // LANES
    for c in range(nchunk):
        qc = q_ref[:, c * LANES:(c + 1) * LANES]
        zero = jnp.zeros_like(qc)
        qs_s[c, :tq, :] = jnp.where(low, qc, zero)
        qs_s[c, tq:, :] = jnp.where(low, zero, qc)
    m_s[...] = jnp.full(m_s.shape, -jnp.inf, F32)
    l_s[...] = jnp.zeros(l_s.shape, F32)
    acc_s[...] = jnp.zeros(acc_s.shape, F32)

    def body(j, carry):
        sl = pl.ds(pl.multiple_of(j * tk, tk), tk)
        kj = k_ref[sl, :]
        vj = v_ref[sl, :]
        for c in range(nchunk):
            s = _dot_nt(qs_s[c], kj)
            m_prev = m_s[c]
            m_next = jnp.maximum(m_prev, jnp.max(s, axis=1, keepdims=True))
            alpha = jnp.exp2(m_prev - m_next)
            ps = [jnp.exp2(s[:, t * LANES:(t + 1) * LANES] - m_next) for t in range(tk // LANES)]
            part = ps[0]
            for pt in ps[1:]:
                part = part + pt
            l_s[c] = alpha * l_s[c] + part
            p = jnp.concatenate(ps, axis=1).astype(BF16)
            acc_s[c] = alpha * acc_s[c] + _dot(p, vj)
            m_s[c] = m_next
        return carry

    lax.fori_loop(0, seq // tk, body, 0)
    for c in range(nchunk):
        o = acc_s[c] / jnp.sum(l_s[c], axis=1, keepdims=True)
        o_ref[:, c * LANES:(c + 1) * LANES] = jnp.where(low, o[:tq], o[tq:])


def _attn(q, k, v):
    bsz, seq, _ = q.shape
    tq, tk = ATT_TQ, min(ATT_TK, seq)
    assert seq % tq == 0 and seq % tk == 0
    kv = pl.BlockSpec((None, seq, KV_DIM), lambda b, i: (b, 0, 0))
    qo = pl.BlockSpec((None, tq, D_ATT), lambda b, i: (b, i, 0))
    stat = pltpu.VMEM((D_ATT // LANES, 2 * tq, LANES), F32)
    return pl.pallas_call(
        functools.partial(_attn_kernel, tq=tq, tk=tk, seq=seq),
        grid=(bsz, seq // tq),
        in_specs=[qo, kv, kv],
        out_specs=qo,
        out_shape=jax.ShapeDtypeStruct((bsz, seq, D_ATT), F32),
        scratch_shapes=[pltpu.VMEM((D_ATT // LANES, 2 * tq, LANES), BF16), stat, stat, stat],
        compiler_params=_cparams("parallel", "parallel"),
        name="attn",
    )(q, k, v)


def _kv_mem_kernel(m_ref, w_ref, k_out, v_out):
    kv = _dot(m_ref[...].astype(BF16), w_ref[...])
    k_out[...] = kv[:, :D_MODEL].astype(BF16)
    v_out[...] = kv[:, D_MODEL:].astype(BF16)


def _kv_mem(mem, wkv_b):
    rows = mem.shape[0]
    blk = pl.BlockSpec((N_MEM, D_MODEL), lambda i: (i, 0))
    return pl.pallas_call(
        _kv_mem_kernel,
        grid=(rows // N_MEM,),
        in_specs=[blk, _full((D_MODEL, 2 * D_MODEL))],
        out_specs=[blk, blk],
        out_shape=[jax.ShapeDtypeStruct((rows, D_MODEL), BF16)] * 2,
        compiler_params=_cparams("parallel"),
        name="kv_mem",
    )(mem, wkv_b)


def _post_kernel(x_ref, g_ref, hf_ref, hb_ref, ya_ref, km_ref, vm_ref,
                 lg_ref, lb_ref, gl_ref, ga_ref, wo1_ref, wo2_ref, l1g_ref, l1b_ref,
                 wq_ref, wxo_ref, l2g_ref, l2b_ref, x2_out, x2w_out, *, alpha):
    x0 = _layer_norm(x_ref[...], lg_ref[...], lb_ref[...])
    y_lru = (hf_ref[...] + hb_ref[...]) * jax.nn.gelu(g_ref[...], approximate=True)
    y_lru = _row_rms(y_lru, gl_ref[...])
    y_att = _row_rms(ya_ref[...], ga_ref[...])
    y = _dot(y_lru.astype(BF16), wo1_ref[...]) + _dot(y_att.astype(BF16), wo2_ref[...])
    x1 = _layer_norm(alpha * x0 + y, l1g_ref[...], l1b_ref[...])
    xq = _dot(x1.astype(BF16), wq_ref[...]).astype(BF16)
    heads = []
    for h in range(X_HEADS):
        sl = slice(h * X_HEAD_DIM, (h + 1) * X_HEAD_DIM)
        s = _dot_nt(xq[:, sl], km_ref[:, sl]) * (X_HEAD_DIM ** -0.5)
        e = jnp.exp(s - jnp.max(s, axis=1, keepdims=True))
        o = _dot(e.astype(BF16), vm_ref[:, sl]) / jnp.sum(e, axis=1, keepdims=True)
        heads.append(o.astype(BF16))
    o = jnp.concatenate(heads, axis=1)
    x2 = _layer_norm(alpha * x1 + _dot(o, wxo_ref[...]), l2g_ref[...], l2b_ref[...])
    x2_out[...] = x2
    _store_chunks(x2w_out, _pack_halves(x2))


def _post(x, g, h, y_att, kmem, vmem, seq, alpha, ln_g, ln_b, gl, ga, wo1, wo2, l1g, l1b, wq, wxo, l2g, l2b):
    n = x.shape[0]
    t = TOK_TILE
    nseq = seq // t
    tok = lambda w: pl.BlockSpec((t, w), lambda i: (i, 0))
    hdir = lambda d: pl.BlockSpec((None, t, D_LRU), lambda i: (d, i, 0))
    mem = pl.BlockSpec((N_MEM, D_MODEL), lambda i: (i // nseq, 0))
    vec = _full((1, D_MODEL))
    half = _full((1, D_LRU))
    sq = _full((D_MODEL, D_MODEL))
    return pl.pallas_call(
        functools.partial(_post_kernel, alpha=alpha),
        grid=(n // t,),
        in_specs=[tok(D_MODEL), tok(D_LRU), hdir(0), hdir(1), tok(D_ATT), mem, mem,
                  vec, vec, half, half, _full((D_LRU, D_MODEL)), _full((D_ATT, D_MODEL)), vec, vec,
                  sq, sq, vec, vec],
        out_specs=[tok(D_MODEL), pl.BlockSpec((ROW_CHUNKS, t, LANES), lambda i: (0, i, 0))],
        out_shape=[jax.ShapeDtypeStruct((n, D_MODEL), F32), jax.ShapeDtypeStruct((ROW_CHUNKS, n, LANES), U32)],
        compiler_params=_cparams("parallel"),
        name="post",
    )(x, g, h, h, y_att, kmem, vmem, ln_g, ln_b, gl, ga, wo1, wo2, l1g, l1b, wq, wxo, l2g, l2b)


def _first_max(v, n):
    rows = lax.broadcasted_iota(jnp.int32, v.shape, 0)
    m = jnp.max(v, axis=0, keepdims=True)
    idx = jnp.min(jnp.where(v == m, rows, n), axis=0, keepdims=True)
    return m, idx, rows == idx


def _router_kernel(x_ref, rwh_ref, rwl_ref, rb_ref, tri_ref, sg_ref, su_ref, sd_ref,
                   eidx_out, gate_out, rank_out, cnt_out, base_out, cnt_s, *, alpha):
    step = pl.program_id(0)

    @pl.when(step == 0)
    def _():
        cnt_s[...] = jnp.zeros_like(cnt_s)

    x = x_ref[...]
    xh, xl = _split_bf16(x)
    logits = _dot_nt(rwh_ref[...], xh) + _dot_nt(rwh_ref[...], xl) + _dot_nt(rwl_ref[...], xh)
    scores = jax.nn.sigmoid(logits)
    biased = scores + rb_ref[...]
    t = x.shape[0]
    neg = -jnp.inf

    gscore = []
    for g in range(N_GROUPS):
        vg = biased[g * GROUP_SIZE:(g + 1) * GROUP_SIZE, :]
        m1, _, hit = _first_max(vg, GROUP_SIZE)
        m2 = jnp.max(jnp.where(hit, neg, vg), axis=0, keepdims=True)
        gscore.append(m1 + m2)
    gs = jnp.concatenate(gscore, axis=0)
    gsel = jnp.zeros(gs.shape, jnp.bool_)
    for _ in range(TOPK_GROUPS):
        _, _, hit = _first_max(gs, N_GROUPS)
        gsel = jnp.logical_or(gsel, hit)
        gs = jnp.where(hit, neg, gs)
    emask = jnp.concatenate(
        [jnp.broadcast_to(gsel[g:g + 1, :], (GROUP_SIZE, t)) for g in range(N_GROUPS)], axis=0)
    masked = jnp.where(emask, biased, neg)

    idxs, gates, hits = [], [], []
    sel = jnp.zeros(masked.shape, jnp.bool_)
    for _ in range(TOP_K):
        _, idx, hit = _first_max(masked, N_EXPERTS)
        idxs.append(idx)
        hits.append(hit)
        gates.append(jnp.sum(jnp.where(hit, scores, 0.0), axis=0, keepdims=True))
        sel = jnp.logical_or(sel, hit)
        masked = jnp.where(hit, neg, masked)
    gsum = gates[0]
    for gk in gates[1:]:
        gsum = gsum + gk
    gate = jnp.concatenate(gates, axis=0) / gsum * ROUTED_SCALE

    chosen = jnp.where(sel, 1.0, 0.0)
    before = _dot(chosen.astype(BF16), tri_ref[...]) + cnt_s[...]
    ranks = [jnp.sum(jnp.where(hit, before, 0.0), axis=0, keepdims=True) for hit in hits]
    cnt = cnt_s[...] + jnp.sum(chosen, axis=1, keepdims=True)
    cnt_s[...] = cnt

    eidx_out[...] = jnp.concatenate(idxs, axis=0)
    gate_out[...] = gate
    rank_out[...] = jnp.concatenate(ranks, axis=0).astype(jnp.int32)
    cnt_out[...] = cnt.astype(jnp.int32)

    hmid = jax.nn.silu(_dot(xh, sg_ref[...])) * _dot(xh, su_ref[...])
    base_out[...] = alpha * x + _dot(hmid.astype(BF16), sd_ref[...])


def _router(x2, rwh, rwl, rb, tri, sg, su, sd, alpha):
    n = x2.shape[0]
    t = TOK_TILE
    tok = pl.BlockSpec((t, D_MODEL), lambda i: (i, 0))
    kt = pl.BlockSpec((TOP_K, t), lambda i: (0, i))
    return pl.pallas_call(
        functools.partial(_router_kernel, alpha=alpha),
        grid=(n // t,),
        in_specs=[tok, _full((N_EXPERTS, D_MODEL)), _full((N_EXPERTS, D_MODEL)), _full((N_EXPERTS, 1)),
                  _full((t, t)), _full((D_MODEL, D_EXPERT)), _full((D_MODEL, D_EXPERT)), _full((D_EXPERT, D_MODEL))],
        out_specs=[kt, kt, kt, _full((N_EXPERTS, 1)), tok],
        out_shape=[jax.ShapeDtypeStruct((TOP_K, n), jnp.int32), jax.ShapeDtypeStruct((TOP_K, n), F32),
                   jax.ShapeDtypeStruct((TOP_K, n), jnp.int32), jax.ShapeDtypeStruct((N_EXPERTS, 1), jnp.int32),
                   jax.ShapeDtypeStruct((n, D_MODEL), F32)],
        scratch_shapes=[pltpu.VMEM((N_EXPERTS, 1), F32)],
        compiler_params=_cparams("arbitrary"),
        name="router",
    )(x2, rwh, rwl, rb, tri, sg, su, sd)


def _sc_mesh():
    return plsc.VectorSubcoreMesh(core_axis_name="core", subcore_axis_name="subcore")


def _sc_worker_base(per_worker):
    return (lax.axis_index("subcore") * SC_CORES + lax.axis_index("core")) * per_worker


def _sc_dispatch(x_sub, sidx, rows_out):
    nsub = x_sub.shape[0]
    win = SC_WINDOW
    per_worker = nsub // (SC_CORES * SC_SUBCORES)
    assert per_worker * SC_CORES * SC_SUBCORES == nsub and per_worker % win == 0

    def body(x_hbm, idx_hbm, out_hbm, idx_v, rows_v, sem):
        base = _sc_worker_base(per_worker)

        @pl.loop(0, per_worker // win)
        def _(ci):
            off = base + ci * win
            pltpu.sync_copy(idx_hbm.at[:, pl.ds(off, win)], idx_v)
            pltpu.sync_copy(x_hbm.at[pl.ds(off, win)], rows_v)
            copies = [pltpu.async_copy(rows_v, out_hbm.at[idx_v.at[k]], sem) for k in range(TOP_K)]
            for cp in copies:
                cp.wait()

    return pl.kernel(
        body, mesh=_sc_mesh(), out_type=jax.ShapeDtypeStruct((rows_out, LANES), U32),
        scratch_types=[pltpu.VMEM((TOP_K, win), jnp.int32), pltpu.VMEM((win, LANES), U32), pltpu.SemaphoreType.DMA],
    )(x_sub, sidx)


def _sc_gather(y_sub, sidx):
    nsub = sidx.shape[1]
    win = SC_WINDOW
    half = TOP_K // 2
    per_worker = nsub // (SC_CORES * SC_SUBCORES)
    assert per_worker * SC_CORES * SC_SUBCORES == nsub and per_worker % win == 0

    def body(y_hbm, idx_hbm, out_hbm, idx_v, rows_v, gsem, osem):
        base = _sc_worker_base(per_worker)

        @pl.loop(0, per_worker // win)
        def _(ci):
            off = base + ci * win
            pltpu.sync_copy(idx_hbm.at[:, pl.ds(off, win)], idx_v)
            for h in range(TOP_K // half):
                gathers = [pltpu.async_copy(y_hbm.at[idx_v.at[h * half + k]], rows_v.at[k], gsem) for k in range(half)]
                for cp in gathers:
                    cp.wait()
                stores = [pltpu.async_copy(rows_v.at[k], out_hbm.at[h * half + k, pl.ds(off, win)], osem)
                          for k in range(half)]
                for cp in stores:
                    cp.wait()

    return pl.kernel(
        body, mesh=_sc_mesh(), out_type=jax.ShapeDtypeStruct((TOP_K, nsub, LANES), U32),
        scratch_types=[pltpu.VMEM((TOP_K, win), jnp.int32), pltpu.VMEM((half, win, LANES), U32),
                       pltpu.SemaphoreType.DMA, pltpu.SemaphoreType.DMA],
    )(y_sub, sidx)


def _experts_kernel(be_ref, nv_ref, xs_ref, *refs):
    weights, y_out = refs[:-1], refs[-1]
    r = MOE_ROWS

    @pl.when(pl.program_id(0) * EXPERT_BLOCKS_PER_STEP < nv_ref[0])
    def _():
        for sub in range(EXPERT_BLOCKS_PER_STEP):
            wg_ref, wu_ref, wd_ref = weights[3 * sub:3 * sub + 3]
            rows = slice(sub * r, (sub + 1) * r)
            lo, hi = _unpack_halves(jnp.concatenate([xs_ref[j, rows, :] for j in range(ROW_CHUNKS)], axis=1))
            lo = lo.astype(BF16)
            hi = hi.astype(BF16)
            gate = _dot(lo, wg_ref[:PACKED, :]) + _dot(hi, wg_ref[PACKED:, :])
            up = _dot(lo, wu_ref[:PACKED, :]) + _dot(hi, wu_ref[PACKED:, :])
            hmid = jax.nn.silu(gate) * up
            y = _pack_halves(_dot(hmid.astype(BF16), wd_ref[...]))
            for j in range(ROW_CHUNKS):
                y_out[j, rows, :] = y[:, j * LANES:(j + 1) * LANES]

    @pl.when(pl.program_id(0) * EXPERT_BLOCKS_PER_STEP >= nv_ref[0])
    def _():
        y_out[...] = jnp.zeros_like(y_out)


def _experts(block_e, nvalid, xs, wg, wu, wd):
    rows = xs.shape[1]
    nsub = EXPERT_BLOCKS_PER_STEP
    step_rows = MOE_ROWS * nsub
    assert rows % step_rows == 0
    blk = pl.BlockSpec((ROW_CHUNKS, step_rows, LANES), lambda i, be, nv: (0, i, 0))
    w_specs = []
    for sub in range(nsub):
        pick = lambda i, be, nv, sub=sub: (be[i * nsub + sub], 0, 0)
        w_specs += [pl.BlockSpec((None, D_MODEL, D_EXPERT), pick), pl.BlockSpec((None, D_MODEL, D_EXPERT), pick),
                    pl.BlockSpec((None, D_EXPERT, D_MODEL), pick)]
    return pl.pallas_call(
        _experts_kernel,
        grid_spec=pltpu.PrefetchScalarGridSpec(
            num_scalar_prefetch=2,
            grid=(rows // step_rows,),
            in_specs=[blk] + w_specs,
            out_specs=blk,
        ),
        out_shape=jax.ShapeDtypeStruct((ROW_CHUNKS, rows, LANES), U32),
        compiler_params=_cparams("arbitrary"),
        name="experts",
    )(block_e, nvalid, xs, *([wg, wu, wd] * nsub))


def _combine_kernel(base_ref, yg_ref, gate_ref, lg_ref, lb_ref, o_ref):
    gate = gate_ref[...]
    routed = None
    for k in range(TOP_K):
        lo, hi = _unpack_halves(_load_chunks(yg_ref.at[k]))
        yk = jnp.concatenate([lo, hi], axis=1) * gate[:, k:k + 1]
        routed = yk if routed is None else routed + yk
    o_ref[...] = _layer_norm(base_ref[...] + routed, lg_ref[...], lb_ref[...])


def _combine(base, yg, gate_t, lg, lb):
    n = base.shape[0]
    t = TOK_TILE
    tok = pl.BlockSpec((t, D_MODEL), lambda i: (i, 0))
    return pl.pallas_call(
        _combine_kernel,
        grid=(n // t,),
        in_specs=[tok, pl.BlockSpec((TOP_K, ROW_CHUNKS, t, LANES), lambda i: (0, 0, i, 0)),
                  pl.BlockSpec((t, TOP_K), lambda i: (i, 0)), _full((1, D_MODEL)), _full((1, D_MODEL))],
        out_specs=tok,
        out_shape=jax.ShapeDtypeStruct((n, D_MODEL), F32),
        compiler_params=_cparams("parallel"),
        name="combine",
    )(base, yg, gate_t, lg, lb)


def _q_lane_perm():
    order = []
    for c in range(Q_GROUP):
        for g in range(KV_HEADS):
            order.append(g * Q_GROUP + c)
    idx = []
    for h in order:
        idx.extend(range(h * HEAD_DIM, (h + 1) * HEAD_DIM))
    return jnp.asarray(idx, jnp.int32)


def _block_diag(w):
    nb, bi, bo = w.shape
    eye = jnp.eye(nb, dtype=w.dtype)
    return (eye[:, None, :, None] * w[:, :, None, :]).reshape(nb * bi, nb * bo)


def _head_avg(width):
    h = jnp.arange(width) // HEAD_DIM
    return jnp.where(h[:, None] == h[None, :], 1.0 / HEAD_DIM, 0.0).astype(BF16)


def _rope_tables(seq):
    rows = seq // GRID_W
    row_id, col_id = jnp.meshgrid(jnp.arange(rows), jnp.arange(GRID_W), indexing='ij')
    row_id = row_id.reshape(-1).astype(F32)
    col_id = col_id.reshape(-1).astype(F32)
    inv_freq = ROPE_THETA ** (-jnp.arange(ROPE_AXIS_FREQS, dtype=F32) / ROPE_AXIS_FREQS)
    ang = jnp.concatenate([row_id[:, None] * inv_freq, col_id[:, None] * inv_freq], axis=-1)
    cos, sin = jnp.cos(ang), jnp.sin(ang)
    reps = LANES // HEAD_DIM
    return jnp.tile(jnp.concatenate([cos, cos], axis=1), (1, reps)), jnp.tile(jnp.concatenate([-sin, sin], axis=1), (1, reps))


def _prep_layer(p):
    perm = _q_lane_perm()
    w_in = p['w_in']
    q0 = 2 * D_LRU
    w_in = jnp.concatenate([w_in[:, :q0], w_in[:, q0:q0 + D_ATT][:, perm], w_in[:, q0 + D_ATT:]], axis=1)
    w_out = p['w_out']
    row = lambda v: v.reshape(1, -1)
    rw_t = p['router_w'].T
    rwh = rw_t.astype(BF16)
    return dict(
        w_in=w_in.astype(BF16),
        gq=jnp.tile(p['q_norm_g'], ATT_HEADS).reshape(1, D_ATT),
        gk=jnp.tile(p['k_norm_g'], KV_HEADS).reshape(1, KV_DIM),
        conv_w=p['conv_w'], conv_b=row(p['conv_b']),
        w_gates=jnp.stack([jnp.concatenate([_block_diag(p['lru_wa'][d]), _block_diag(p['lru_wx'][d])], axis=1)
                           for d in range(2)]).astype(BF16),
        ba=p['lru_ba'].reshape(2, 1, D_LRU), bx=p['lru_bx'].reshape(2, 1, D_LRU),
        lam=p['lru_lambda'].reshape(2, 1, D_LRU),
        gl=row(p['gn_lru_g']), ga=row(p['gn_att_g'][perm]),
        wo1=w_out[:D_LRU].astype(BF16), wo2=w_out[D_LRU:][perm].astype(BF16),
        l1g=row(p['ln1_g']), l1b=row(p['ln1_b']),
        wq=p['xa_wq'].astype(BF16), wkv=p['xa_wkv'].astype(BF16), wxo=p['xa_wo'].astype(BF16),
        l2g=row(p['ln2_g']), l2b=row(p['ln2_b']),
        rwh=rwh, rwl=(rw_t - rwh.astype(F32)).astype(BF16), rb=p['router_b'].reshape(N_EXPERTS, 1),
        wg=p['w_gate'].astype(BF16), wu=p['w_up'].astype(BF16), wd=p['w_down'].astype(BF16),
        sg=p['sh_gate'].astype(BF16), su=p['sh_up'].astype(BF16), sd=p['sh_down'].astype(BF16),
        l3g=row(p['ln3_g']), l3b=row(p['ln3_b']),
    )


def _moe(x2, x2w, lp, alpha, tri):
    n = x2.shape[0]
    eidx, gate, rank, counts, base = _router(x2, lp['rwh'], lp['rwl'], lp['rb'], tri,
                                             lp['sg'], lp['su'], lp['sd'], alpha)
    r = MOE_ROWS
    n_blocks = (n * TOP_K) // r + N_EXPERTS
    rows = n_blocks * r
    counts = counts.reshape(N_EXPERTS)
    pad_counts = ((counts + r - 1) // r) * r
    pad_end = jnp.cumsum(pad_counts)
    pad_start = pad_end - pad_counts
    start_of = jnp.sum(jnp.where(eidx[:, :, None] == jnp.arange(N_EXPERTS), pad_start, 0), axis=-1)
    dest = start_of + rank
    block_e = jnp.minimum(jnp.sum(pad_end[None, :] // r <= jnp.arange(n_blocks)[:, None], axis=1),
                          N_EXPERTS - 1).astype(jnp.int32)
    nvalid = (pad_end[-1] // r).astype(jnp.int32).reshape(1)
    sidx = (dest[:, None, :] + (jnp.arange(ROW_CHUNKS, dtype=jnp.int32) * rows)[None, :, None]).reshape(TOP_K, ROW_CHUNKS * n)
    xs = _sc_dispatch(x2w.reshape(ROW_CHUNKS * n, LANES), sidx, ROW_CHUNKS * rows)
    ys = _experts(block_e, nvalid, xs.reshape(ROW_CHUNKS, rows, LANES), lp['wg'], lp['wu'], lp['wd'])
    yg = _sc_gather(ys.reshape(ROW_CHUNKS * rows, LANES), sidx)
    return base, yg.reshape(TOP_K, ROW_CHUNKS, n, LANES), gate.T


def _layer(x, mem, lp, alpha, consts, first):
    bsz, seq, _ = x.shape
    n = bsz * seq
    xf = x.reshape(n, D_MODEL)
    cos_t, sin_t = _rope_tables(seq)
    g, xbr, q, k, v = _in_proj(xf, seq, consts['ln_g'], consts['ln_b'], lp['w_in'], lp['gq'], lp['gk'],
                               cos_t, sin_t, consts['avgq'], consts['avgk'])
    h = _lru(xbr.reshape(bsz, seq, D_LRU), lp['conv_w'], lp['conv_b'], lp['w_gates'], lp['ba'], lp['bx'], lp['lam'])
    y_att = _attn(q.reshape(bsz, seq, D_ATT), k.reshape(bsz, seq, KV_DIM), v.reshape(bsz, seq, KV_DIM))
    kmem, vmem = _kv_mem(mem.reshape(bsz * N_MEM, D_MODEL), lp['wkv'])
    x2, x2w = _post(xf, g, h.reshape(2, n, D_LRU), y_att.reshape(n, D_ATT), kmem, vmem, seq, alpha,
                    consts['ln_g'], consts['ln_b'], lp['gl'], lp['ga'], lp['wo1'], lp['wo2'], lp['l1g'], lp['l1b'],
                    lp['wq'], lp['wxo'], lp['l2g'], lp['l2b'])
    base, yg, gate_t = _moe(x2, x2w, lp, alpha, consts['tri'])
    out = _combine(base, yg, gate_t, lp['l3g'], lp['l3b'])
    return out.reshape(bsz, seq, D_MODEL)


def kernel(x_prompt, x_sample, mem_prompt, mem_sample, ln_in_g, ln_in_b, w_in, conv_w, conv_b, lru_wa, lru_ba, lru_wx, lru_bx, lru_lambda, q_norm_g, k_norm_g, gn_lru_g, gn_att_g, w_out, ln1_g, ln1_b, xa_wq, xa_wkv, xa_wo, ln2_g, ln2_b, router_w, router_b, w_gate, w_up, w_down, sh_gate, sh_up, sh_down, ln3_g, ln3_b):
    depth = w_in.shape[0]
    assert depth == 1, "the fused in_proj/post kernels assume the input LayerNorm feeds a single layer"
    alpha = (2.0 * depth) ** 0.25
    stacked = dict(w_in=w_in, conv_w=conv_w, conv_b=conv_b, lru_wa=lru_wa, lru_ba=lru_ba, lru_wx=lru_wx,
                   lru_bx=lru_bx, lru_lambda=lru_lambda, q_norm_g=q_norm_g, k_norm_g=k_norm_g, gn_lru_g=gn_lru_g,
                   gn_att_g=gn_att_g, w_out=w_out, ln1_g=ln1_g, ln1_b=ln1_b, xa_wq=xa_wq, xa_wkv=xa_wkv,
                   xa_wo=xa_wo, ln2_g=ln2_g, ln2_b=ln2_b, router_w=router_w, router_b=router_b, w_gate=w_gate,
                   w_up=w_up, w_down=w_down, sh_gate=sh_gate, sh_up=sh_up, sh_down=sh_down, ln3_g=ln3_g, ln3_b=ln3_b)
    lp = _prep_layer({name: val[0] for name, val in stacked.items()})
    tri = (jnp.arange(TOK_TILE)[:, None] < jnp.arange(TOK_TILE)[None, :]).astype(BF16)
    consts = dict(ln_g=ln_in_g.reshape(1, D_MODEL), ln_b=ln_in_b.reshape(1, D_MODEL),
                  avgq=_head_avg(min(D_ATT, MXU_DIM)), avgk=_head_avg(min(KV_DIM, MXU_DIM)), tri=tri)
    y_prompt = _layer(x_prompt, mem_prompt, lp, alpha, consts, True)
    y_sample = _layer(x_sample, mem_sample, lp, alpha, consts, False)
    return (y_prompt, y_sample)
```

```python
import functools

import jax
import jax.numpy as jnp
from jax import lax
from jax.experimental import pallas as pl
from jax.experimental.pallas import tpu as pltpu
from jax.experimental.pallas import tpu_sc as plsc

F32 = jnp.float32
BF16 = jnp.bfloat16
U32 = jnp.uint32

D_MODEL = 1024
HEAD_DIM = 64
D_LRU = 512
LRU_BLOCKS = 8
LRU_C = 8.0
CONV_W = 4
CONV_LEFT = 2
ATT_HEADS = 8
KV_HEADS = 2
Q_GROUP = ATT_HEADS // KV_HEADS
D_ATT = ATT_HEADS * HEAD_DIM
KV_DIM = KV_HEADS * HEAD_DIM
D_IN = 2 * D_LRU + D_ATT + 2 * KV_DIM
GRID_W = 64
ROPE_THETA = 10000.0
ROPE_AXIS_FREQS = HEAD_DIM // 4
N_MEM = 256
X_HEADS = 4
X_HEAD_DIM = D_MODEL // X_HEADS
N_EXPERTS = 64
TOP_K = 8
N_GROUPS = 8
GROUP_SIZE = N_EXPERTS // N_GROUPS
TOPK_GROUPS = 4
D_EXPERT = 256
PACKED = D_MODEL // 2
ROUTED_SCALE = 2.5
LN_EPS = 1e-5
RMS_EPS = 1e-6
LOG2_E = 1.4426950408889634

LANES = 128
SUBLANES = 8
MXU_DIM = 256
TOK_TILE = 512
LRU_TILE = 512
ATT_TQ = 256
ATT_TK = 2048
MOE_ROWS = 512
EXPERT_BLOCKS_PER_STEP = 2
ROW_CHUNKS = PACKED // LANES
SC_CORES = 2
SC_SUBCORES = 16
SC_WINDOW = 128
VMEM_LIMIT = 48 * 1024 * 1024


def _cparams(*sem):
    return pltpu.CompilerParams(dimension_semantics=sem, vmem_limit_bytes=VMEM_LIMIT)


def _full(shape):
    return pl.BlockSpec(shape, lambda *_: (0,) * len(shape))


def _layer_norm(x, g, b):
    mu = jnp.mean(x, axis=-1, keepdims=True)
    xc = x - mu
    var = jnp.mean(xc * xc, axis=-1, keepdims=True)
    return xc * lax.rsqrt(var + LN_EPS) * g + b


def _row_rms(x, g):
    return x * lax.rsqrt(jnp.mean(x * x, axis=-1, keepdims=True) + RMS_EPS) * g


def _dot(a, b):
    return jnp.dot(a, b, preferred_element_type=F32)


def _dot_nt(a, b):
    return lax.dot_general(a, b, (((1,), (1,)), ((), ())), preferred_element_type=F32)


def _pack_halves(x):
    w = x.shape[1] // 2
    lo = lax.bitcast_convert_type(x[:, :w].astype(BF16).astype(F32), U32)
    hi = lax.bitcast_convert_type(x[:, w:].astype(BF16).astype(F32), U32)
    return (lo >> 16) | (hi & jnp.uint32(0xFFFF0000))


def _unpack_halves(w):
    lo = lax.bitcast_convert_type(w << 16, F32)
    hi = lax.bitcast_convert_type(w & jnp.uint32(0xFFFF0000), F32)
    return lo, hi


def _store_chunks(ref, w):
    for j in range(ROW_CHUNKS):
        ref[j] = w[:, j * LANES:(j + 1) * LANES]


def _load_chunks(ref):
    return jnp.concatenate([ref[j] for j in range(ROW_CHUNKS)], axis=1)


def _split_bf16(x):
    hi = x.astype(BF16)
    lo = (x - hi.astype(F32)).astype(BF16)
    return hi, lo


def _head_mean_square(z, avg_ref):
    hi, lo = _split_bf16(z * z)
    avg = avg_ref[...]
    w = avg.shape[0]
    parts = [_dot(hi[:, c:c + w], avg) + _dot(lo[:, c:c + w], avg) for c in range(0, z.shape[1], w)]
    return parts[0] if len(parts) == 1 else jnp.concatenate(parts, axis=1)


def _rope(x, cos, sin_signed):
    lane = lax.broadcasted_iota(jnp.int32, (1, LANES), 1)
    low_half = (lane % HEAD_DIM) < (HEAD_DIM // 2)
    outs = []
    for c in range(x.shape[1] // LANES):
        xc = x[:, c * LANES:(c + 1) * LANES]
        partner = jnp.where(low_half, pltpu.roll(xc, LANES - HEAD_DIM // 2, 1), pltpu.roll(xc, HEAD_DIM // 2, 1))
        outs.append(xc * cos + partner * sin_signed)
    return outs[0] if len(outs) == 1 else jnp.concatenate(outs, axis=1)


def _in_proj_kernel(x_ref, lg_ref, lb_ref, w_ref, gq_ref, gk_ref, cos_ref, sin_ref, avgq_ref, avgk_ref,
                    g_out, xbr_out, q_out, k_out, v_out):
    xn = _layer_norm(x_ref[...], lg_ref[...], lb_ref[...])
    z = _dot(xn.astype(BF16), w_ref[...])
    g_out[...] = z[:, :D_LRU]
    xbr_out[...] = z[:, D_LRU:2 * D_LRU]
    zq = z[:, 2 * D_LRU:2 * D_LRU + D_ATT]
    zk = z[:, 2 * D_LRU + D_ATT:2 * D_LRU + D_ATT + KV_DIM]
    v_out[...] = z[:, 2 * D_LRU + D_ATT + KV_DIM:].astype(BF16)
    cos = cos_ref[...]
    sin = sin_ref[...]
    q = zq * lax.rsqrt(_head_mean_square(zq, avgq_ref) + RMS_EPS) * gq_ref[...]
    k = zk * lax.rsqrt(_head_mean_square(zk, avgk_ref) + RMS_EPS) * gk_ref[...]
    q_out[...] = (_rope(q, cos, sin) * (HEAD_DIM ** -0.5 * LOG2_E)).astype(BF16)
    k_out[...] = _rope(k, cos, sin).astype(BF16)


def _in_proj(x, seq, ln_g, ln_b, w_in_b, gq, gk, cos_t, sin_t, avgq, avgk):
    n = x.shape[0]
    t = TOK_TILE
    nseq = seq // t
    tok = lambda w: pl.BlockSpec((t, w), lambda i: (i, 0))
    pos = pl.BlockSpec((t, LANES), lambda i: (i % nseq, 0))
    return pl.pallas_call(
        _in_proj_kernel,
        grid=(n // t,),
        in_specs=[tok(D_MODEL), _full((1, D_MODEL)), _full((1, D_MODEL)), _full((D_MODEL, D_IN)),
                  _full((1, D_ATT)), _full((1, KV_DIM)), pos, pos, _full(avgq.shape), _full(avgk.shape)],
        out_specs=[tok(D_LRU), tok(D_LRU), tok(D_ATT), tok(KV_DIM), tok(KV_DIM)],
        out_shape=[jax.ShapeDtypeStruct((n, D_LRU), F32), jax.ShapeDtypeStruct((n, D_LRU), F32),
                   jax.ShapeDtypeStruct((n, D_ATT), BF16), jax.ShapeDtypeStruct((n, KV_DIM), BF16),
                   jax.ShapeDtypeStruct((n, KV_DIM), BF16)],
        compiler_params=_cparams("parallel"),
        name="in_proj",
    )(x, ln_g, ln_b, w_in_b, gq, gk, cos_t, sin_t, avgq, avgk)


def _softplus(x):
    return jnp.maximum(x, 0.0) + jnp.log1p(jnp.exp(-jnp.abs(x)))


def _lru_kernel(xp_ref, x_ref, xn_ref, cw_ref, cb_ref, w_ref, ba_ref, bx_ref, lam_ref, h_ref,
                a_s, b_s, c_s, *, tb, nt):
    d = pl.program_id(1)
    i = pl.program_id(2)
    tbi = jnp.where(d == 0, i, nt - 1 - i)
    x = x_ref[...]
    prev = jnp.where(tbi > 0, xp_ref[...], 0.0)
    nxt = jnp.where(tbi < nt - 1, xn_ref[...], 0.0)
    xe = jnp.concatenate([prev, x, nxt], axis=0)
    cw = cw_ref[...]
    o = SUBLANES - CONV_LEFT
    xc = xe[o:o + tb] * cw[0:1]
    for k in range(1, CONV_W):
        xc = xc + xe[o + k:o + k + tb] * cw[k:k + 1]
    xc = xc + cb_ref[...]
    rg = _dot(xc.astype(BF16), w_ref[...])
    r = jax.nn.sigmoid(rg[:, :D_LRU] + ba_ref[...])
    ig = jax.nn.sigmoid(rg[:, D_LRU:] + bx_ref[...])
    log_a = -LRU_C * r * _softplus(-lam_ref[...])
    a = jnp.exp(log_a)
    b = jnp.sqrt(-jnp.tanh(log_a) * (a * a + 1.0)) * (ig * xc)

    ngroups = tb // SUBLANES
    row = lax.broadcasted_iota(jnp.int32, (ngroups, SUBLANES, D_LRU), 1)

    @pl.when(i == 0)
    def _():
        c_s[...] = jnp.zeros_like(c_s)

    def scan(reverse):
        aa = a.reshape(ngroups, SUBLANES, D_LRU)
        bb = b.reshape(ngroups, SUBLANES, D_LRU)
        for k in (1, 2, 4):
            if reverse:
                valid = row < SUBLANES - k
                shift = SUBLANES - k
            else:
                valid = row >= k
                shift = k
            a_sh = jnp.where(valid, pltpu.roll(aa, shift, 1), 1.0)
            b_sh = jnp.where(valid, pltpu.roll(bb, shift, 1), 0.0)
            bb = aa * b_sh + bb
            aa = aa * a_sh
        a_s[...] = aa.reshape(tb, D_LRU)
        b_s[...] = bb.reshape(tb, D_LRU)

        def body(gi, c):
            g = (ngroups - 1 - gi) if reverse else gi
            sl = pl.ds(pl.multiple_of(g * SUBLANES, SUBLANES), SUBLANES)
            h = a_s[sl, :] * c + b_s[sl, :]
            h_ref[sl, :] = h
            edge = h[0:1, :] if reverse else h[SUBLANES - 1:SUBLANES, :]
            return jnp.broadcast_to(edge, (SUBLANES, D_LRU))

        c_s[...] = lax.fori_loop(0, ngroups, body, c_s[...], unroll=8)

    @pl.when(d == 0)
    def _():
        scan(False)

    @pl.when(d == 1)
    def _():
        scan(True)


def _lru(xbr, conv_w, conv_b, w_gates, ba, bx, lam):
    bsz, seq, _ = xbr.shape
    tb = LRU_TILE
    nt = seq // tb
    r8 = tb // SUBLANES
    nb8 = seq // SUBLANES

    def tblk(d, i):
        return jnp.where(d == 0, i, nt - 1 - i)

    cur = pl.BlockSpec((None, tb, D_LRU), lambda b, d, i: (b, tblk(d, i), 0))
    prv = pl.BlockSpec((None, SUBLANES, D_LRU), lambda b, d, i: (b, jnp.maximum(tblk(d, i) * r8 - 1, 0), 0))
    nxt = pl.BlockSpec((None, SUBLANES, D_LRU), lambda b, d, i: (b, jnp.minimum((tblk(d, i) + 1) * r8, nb8 - 1), 0))
    per_dir = lambda shape: pl.BlockSpec((None,) + shape, lambda b, d, i: (d,) + (0,) * len(shape))
    return pl.pallas_call(
        functools.partial(_lru_kernel, tb=tb, nt=nt),
        grid=(bsz, 2, nt),
        in_specs=[prv, cur, nxt, _full((CONV_W, D_LRU)), _full((1, D_LRU)),
                  per_dir((D_LRU, 2 * D_LRU)), per_dir((1, D_LRU)), per_dir((1, D_LRU)), per_dir((1, D_LRU))],
        out_specs=pl.BlockSpec((None, None, tb, D_LRU), lambda b, d, i: (d, b, tblk(d, i), 0)),
        out_shape=jax.ShapeDtypeStruct((2, bsz, seq, D_LRU), F32),
        scratch_shapes=[pltpu.VMEM((tb, D_LRU), F32), pltpu.VMEM((tb, D_LRU), F32),
                        pltpu.VMEM((SUBLANES, D_LRU), F32)],
        compiler_params=_cparams("parallel", "arbitrary", "arbitrary"),
        name="lru",
    )(xbr, xbr, xbr, conv_w, conv_b, w_gates, ba, bx, lam)


def _attn_kernel(q_ref, k_ref, v_ref, o_ref, qs_s, m_s, l_s, acc_s, *, tq, tk, seq):
    lane = lax.broadcasted_iota(jnp.int32, (1, LANES), 1)
    low = lane < HEAD_DIM
    nchunk = D_ATT // LANES
    for c in range(nchunk):
        qc = q_ref[:, c * LANES:(c + 1) * LANES]
        zero = jnp.zeros_like(qc)
        qs_s[c, :tq, :] = jnp.where(low, qc, zero)
        qs_s[c, tq:, :] = jnp.where(low, zero, qc)
    m_s[...] = jnp.full(m_s.shape, -jnp.inf, F32)
    l_s[...] = jnp.zeros(l_s.shape, F32)
    acc_s[...] = jnp.zeros(acc_s.shape, F32)

    def body(j, carry):
        sl = pl.ds(pl.multiple_of(j * tk, tk), tk)
        kj = k_ref[sl, :]
        vj = v_ref[sl, :]
        for c in range(nchunk):
            s = _dot_nt(qs_s[c], kj)
            m_prev = m_s[c]
            m_next = jnp.maximum(m_prev, jnp.max(s, axis=1, keepdims=True))
            alpha = jnp.exp2(m_prev - m_next)
            ps = [jnp.exp2(s[:, t * LANES:(t + 1) * LANES] - m_next) for t in range(tk // LANES)]
            part = ps[0]
            for pt in ps[1:]:
                part = part + pt
            l_s[c] = alpha * l_s[c] + part
            p = jnp.concatenate(ps, axis=1).astype(BF16)
            acc_s[c] = alpha * acc_s[c] + _dot(p, vj)
            m_s[c] = m_next
        return carry

    lax.fori_loop(0, seq // tk, body, 0)
    for c in range(nchunk):
        o = acc_s[c] / jnp.sum(l_s[c], axis=1, keepdims=True)
        o_ref[:, c * LANES:(c + 1) * LANES] = jnp.where(low, o[:tq], o[tq:])


def _attn(q, k, v):
    bsz, seq, _ = q.shape
    tq, tk = ATT_TQ, min(ATT_TK, seq)
    assert seq % tq == 0 and seq % tk == 0
    kv = pl.BlockSpec((None, seq, KV_DIM), lambda b, i: (b, 0, 0))
    qo = pl.BlockSpec((None, tq, D_ATT), lambda b, i: (b, i, 0))
    stat = pltpu.VMEM((D_ATT // LANES, 2 * tq, LANES), F32)
    return pl.pallas_call(
        functools.partial(_attn_kernel, tq=tq, tk=tk, seq=seq),
        grid=(bsz, seq // tq),
        in_specs=[qo, kv, kv],
        out_specs=qo,
        out_shape=jax.ShapeDtypeStruct((bsz, seq, D_ATT), F32),
        scratch_shapes=[pltpu.VMEM((D_ATT // LANES, 2 * tq, LANES), BF16), stat, stat, stat],
        compiler_params=_cparams("parallel", "parallel"),
        name="attn",
    )(q, k, v)


def _kv_mem_kernel(m_ref, w_ref, k_out, v_out):
    kv = _dot(m_ref[...].astype(BF16), w_ref[...])
    k_out[...] = kv[:, :D_MODEL].astype(BF16)
    v_out[...] = kv[:, D_MODEL:].astype(BF16)


def _kv_mem(mem, wkv_b):
    rows = mem.shape[0]
    blk = pl.BlockSpec((N_MEM, D_MODEL), lambda i: (i, 0))
    return pl.pallas_call(
        _kv_mem_kernel,
        grid=(rows // N_MEM,),
        in_specs=[blk, _full((D_MODEL, 2 * D_MODEL))],
        out_specs=[blk, blk],
        out_shape=[jax.ShapeDtypeStruct((rows, D_MODEL), BF16)] * 2,
        compiler_params=_cparams("parallel"),
        name="kv_mem",
    )(mem, wkv_b)


def _post_kernel(x_ref, g_ref, hf_ref, hb_ref, ya_ref, km_ref, vm_ref,
                 lg_ref, lb_ref, gl_ref, ga_ref, wo1_ref, wo2_ref, l1g_ref, l1b_ref,
                 wq_ref, wxo_ref, l2g_ref, l2b_ref, rwh_ref, rwl_ref, rb_ref, tri_ref, sg_ref, su_ref, sd_ref,
                 x2w_out, eidx_out, gate_out, rank_out, cnt_out, base_out, cnt_s, *, alpha):
    x0 = _layer_norm(x_ref[...], lg_ref[...], lb_ref[...])
    y_lru = (hf_ref[...] + hb_ref[...]) * jax.nn.gelu(g_ref[...], approximate=True)
    y_lru = _row_rms(y_lru, gl_ref[...])
    y_att = _row_rms(ya_ref[...], ga_ref[...])
    y = _dot(y_lru.astype(BF16), wo1_ref[...]) + _dot(y_att.astype(BF16), wo2_ref[...])
    x1 = _layer_norm(alpha * x0 + y, l1g_ref[...], l1b_ref[...])
    xq = _dot(x1.astype(BF16), wq_ref[...]).astype(BF16)
    heads = []
    for h in range(X_HEADS):
        sl = slice(h * X_HEAD_DIM, (h + 1) * X_HEAD_DIM)
        s = _dot_nt(xq[:, sl], km_ref[:, sl]) * (X_HEAD_DIM ** -0.5)
        e = jnp.exp(s - jnp.max(s, axis=1, keepdims=True))
        o = _dot(e.astype(BF16), vm_ref[:, sl]) / jnp.sum(e, axis=1, keepdims=True)
        heads.append(o.astype(BF16))
    o = jnp.concatenate(heads, axis=1)
    x2 = _layer_norm(alpha * x1 + _dot(o, wxo_ref[...]), l2g_ref[...], l2b_ref[...])
    _store_chunks(x2w_out, _pack_halves(x2))
    _route(x2, rwh_ref, rwl_ref, rb_ref, tri_ref, sg_ref, su_ref, sd_ref,
           eidx_out, gate_out, rank_out, cnt_out, base_out, cnt_s, alpha)


def _post(x, g, h, y_att, kmem, vmem, seq, alpha, ln_g, ln_b, gl, ga, wo1, wo2, l1g, l1b, wq, wxo, l2g, l2b,
          rwh, rwl, rb, tri, sg, su, sd):
    n = x.shape[0]
    t = TOK_TILE
    nseq = seq // t
    tok = lambda w: pl.BlockSpec((t, w), lambda i: (i, 0))
    hdir = lambda d: pl.BlockSpec((None, t, D_LRU), lambda i: (d, i, 0))
    mem = pl.BlockSpec((N_MEM, D_MODEL), lambda i: (i // nseq, 0))
    vec = _full((1, D_MODEL))
    half = _full((1, D_LRU))
    sq = _full((D_MODEL, D_MODEL))
    kt = pl.BlockSpec((TOP_K, t), lambda i: (0, i))
    return pl.pallas_call(
        functools.partial(_post_kernel, alpha=alpha),
        grid=(n // t,),
        in_specs=[tok(D_MODEL), tok(D_LRU), hdir(0), hdir(1), tok(D_ATT), mem, mem,
                  vec, vec, half, half, _full((D_LRU, D_MODEL)), _full((D_ATT, D_MODEL)), vec, vec,
                  sq, sq, vec, vec,
                  _full((N_EXPERTS, D_MODEL)), _full((N_EXPERTS, D_MODEL)), _full((N_EXPERTS, 1)), _full((t, t)),
                  _full((D_MODEL, D_EXPERT)), _full((D_MODEL, D_EXPERT)), _full((D_EXPERT, D_MODEL))],
        out_specs=[pl.BlockSpec((ROW_CHUNKS, t, LANES), lambda i: (0, i, 0)), kt, kt, kt,
                   _full((N_EXPERTS, 1)), tok(D_MODEL)],
        out_shape=[jax.ShapeDtypeStruct((ROW_CHUNKS, n, LANES), U32),
                   jax.ShapeDtypeStruct((TOP_K, n), jnp.int32), jax.ShapeDtypeStruct((TOP_K, n), F32),
                   jax.ShapeDtypeStruct((TOP_K, n), jnp.int32), jax.ShapeDtypeStruct((N_EXPERTS, 1), jnp.int32),
                   jax.ShapeDtypeStruct((n, D_MODEL), F32)],
        scratch_shapes=[pltpu.VMEM((N_EXPERTS, 1), F32)],
        compiler_params=_cparams("arbitrary"),
        name="post",
    )(x, g, h, h, y_att, kmem, vmem, ln_g, ln_b, gl, ga, wo1, wo2, l1g, l1b, wq, wxo, l2g, l2b,
      rwh, rwl, rb, tri, sg, su, sd)


def _first_max(v, n):
    rows = lax.broadcasted_iota(jnp.int32, v.shape, 0)
    m = jnp.max(v, axis=0, keepdims=True)
    idx = jnp.min(jnp.where(v == m, rows, n), axis=0, keepdims=True)
    return m, idx, rows == idx


def _route(x, rwh_ref, rwl_ref, rb_ref, tri_ref, sg_ref, su_ref, sd_ref,
           eidx_out, gate_out, rank_out, cnt_out, base_out, cnt_s, alpha):
    @pl.when(pl.program_id(0) == 0)
    def _():
        cnt_s[...] = jnp.zeros_like(cnt_s)

    xh, xl = _split_bf16(x)
    logits = _dot_nt(rwh_ref[...], xh) + _dot_nt(rwh_ref[...], xl) + _dot_nt(rwl_ref[...], xh)
    scores = jax.nn.sigmoid(logits)
    biased = scores + rb_ref[...]
    t = x.shape[0]
    neg = -jnp.inf

    gscore = []
    for g in range(N_GROUPS):
        vg = biased[g * GROUP_SIZE:(g + 1) * GROUP_SIZE, :]
        m1, _, hit = _first_max(vg, GROUP_SIZE)
        m2 = jnp.max(jnp.where(hit, neg, vg), axis=0, keepdims=True)
        gscore.append(m1 + m2)
    gs = jnp.concatenate(gscore, axis=0)
    gsel = jnp.zeros(gs.shape, jnp.bool_)
    for _ in range(TOPK_GROUPS):
        _, _, hit = _first_max(gs, N_GROUPS)
        gsel = jnp.logical_or(gsel, hit)
        gs = jnp.where(hit, neg, gs)
    emask = jnp.concatenate(
        [jnp.broadcast_to(gsel[g:g + 1, :], (GROUP_SIZE, t)) for g in range(N_GROUPS)], axis=0)
    masked = jnp.where(emask, biased, neg)

    idxs, gates, hits = [], [], []
    sel = jnp.zeros(masked.shape, jnp.bool_)
    for _ in range(TOP_K):
        _, idx, hit = _first_max(masked, N_EXPERTS)
        idxs.append(idx)
        hits.append(hit)
        gates.append(jnp.sum(jnp.where(hit, scores, 0.0), axis=0, keepdims=True))
        sel = jnp.logical_or(sel, hit)
        masked = jnp.where(hit, neg, masked)
    gsum = gates[0]
    for gk in gates[1:]:
        gsum = gsum + gk
    gate = jnp.concatenate(gates, axis=0) / gsum * ROUTED_SCALE

    chosen = jnp.where(sel, 1.0, 0.0)
    before = _dot(chosen.astype(BF16), tri_ref[...]) + cnt_s[...]
    ranks = [jnp.sum(jnp.where(hit, before, 0.0), axis=0, keepdims=True) for hit in hits]
    cnt = cnt_s[...] + jnp.sum(chosen, axis=1, keepdims=True)
    cnt_s[...] = cnt

    eidx_out[...] = jnp.concatenate(idxs, axis=0)
    gate_out[...] = gate
    rank_out[...] = jnp.concatenate(ranks, axis=0).astype(jnp.int32)
    cnt_out[...] = cnt.astype(jnp.int32)

    hmid = jax.nn.silu(_dot(xh, sg_ref[...])) * _dot(xh, su_ref[...])
    base_out[...] = alpha * x + _dot(hmid.astype(BF16), sd_ref[...])


def _sc_mesh():
    return plsc.VectorSubcoreMesh(core_axis_name="core", subcore_axis_name="subcore")


def _sc_worker_base(per_worker):
    return (lax.axis_index("subcore") * SC_CORES + lax.axis_index("core")) * per_worker


def _sc_dispatch(x_sub, sidx, rows_out):
    nsub = x_sub.shape[0]
    win = SC_WINDOW
    per_worker = nsub // (SC_CORES * SC_SUBCORES)
    assert per_worker * SC_CORES * SC_SUBCORES == nsub and per_worker % win == 0

    nwin = per_worker // win
    assert nwin % 2 == 0

    def body(x_hbm, idx_hbm, out_hbm, idx_v, rows_v, lsem0, lsem1, ssem0, ssem1):
        base = _sc_worker_base(per_worker)
        lsem = (lsem0, lsem1)
        ssem = (ssem0, ssem1)

        def loads(ci, b):
            off = base + ci * win
            return (pltpu.make_async_copy(idx_hbm.at[:, pl.ds(off, win)], idx_v.at[b], lsem[b]),
                    pltpu.make_async_copy(x_hbm.at[pl.ds(off, win)], rows_v.at[b], lsem[b]))

        def scatters(b):
            return [pltpu.make_async_copy(rows_v.at[b], out_hbm.at[idx_v.at[b, k]], ssem[b]) for k in range(TOP_K)]

        for cp in loads(0, 0):
            cp.start()

        @pl.loop(0, nwin, step=2)
        def _(c0):
            for b in range(2):
                ci = c0 + b
                for cp in loads(ci, b):
                    cp.wait()
                for cp in scatters(b):
                    cp.start()

                @pl.when(ci >= 1)
                def _():
                    for cp in scatters(1 - b):
                        cp.wait()

                @pl.when(ci + 1 < nwin)
                def _():
                    for cp in loads(ci + 1, 1 - b):
                        cp.start()

        for cp in scatters(1):
            cp.wait()

    return pl.kernel(
        body, mesh=_sc_mesh(), out_type=jax.ShapeDtypeStruct((rows_out, LANES), U32),
        scratch_types=[pltpu.VMEM((2, TOP_K, win), jnp.int32), pltpu.VMEM((2, win, LANES), U32)]
        + [pltpu.SemaphoreType.DMA] * 4,
    )(x_sub, sidx)


def _sc_gather(y_sub, sidx):
    nsub = sidx.shape[1]
    win = SC_WINDOW
    pair = 2
    npairs = TOP_K // pair
    per_worker = nsub // (SC_CORES * SC_SUBCORES)
    assert per_worker * SC_CORES * SC_SUBCORES == nsub and per_worker % win == 0

    def body(y_hbm, idx_hbm, out_hbm, idx_v, rows_v, gsem, osem0, osem1):
        base = _sc_worker_base(per_worker)
        osem = (osem0, osem1)

        def gathers(q, b):
            return [pltpu.make_async_copy(y_hbm.at[idx_v.at[q * pair + j]], rows_v.at[b, j], gsem) for j in range(pair)]

        def stores(off, q, b):
            return [pltpu.make_async_copy(rows_v.at[b, j], out_hbm.at[q * pair + j, pl.ds(off, win)], osem[b])
                    for j in range(pair)]

        @pl.loop(0, per_worker // win)
        def _(ci):
            off = base + ci * win
            pltpu.sync_copy(idx_hbm.at[:, pl.ds(off, win)], idx_v)
            for q in range(npairs):
                b = q % 2
                if q >= 2:
                    for cp in stores(off, q - 2, b):
                        cp.wait()
                copies = gathers(q, b)
                for cp in copies:
                    cp.start()
                for cp in copies:
                    cp.wait()
                for cp in stores(off, q, b):
                    cp.start()
            for q in (npairs - 2, npairs - 1):
                for cp in stores(off, q, q % 2):
                    cp.wait()

    return pl.kernel(
        body, mesh=_sc_mesh(), out_type=jax.ShapeDtypeStruct((TOP_K, nsub, LANES), U32),
        scratch_types=[pltpu.VMEM((TOP_K, win), jnp.int32), pltpu.VMEM((2, pair, win, LANES), U32)]
        + [pltpu.SemaphoreType.DMA] * 3,
    )(y_sub, sidx)


def _experts_kernel(be_ref, nv_ref, xs_ref, *refs):
    weights, y_out = refs[:-1], refs[-1]
    r = MOE_ROWS

    @pl.when(pl.program_id(0) * EXPERT_BLOCKS_PER_STEP < nv_ref[0])
    def _():
        for sub in range(EXPERT_BLOCKS_PER_STEP):
            wg_ref, wu_ref, wd_ref = weights[3 * sub:3 * sub + 3]
            rows = slice(sub * r, (sub + 1) * r)
            lo, hi = _unpack_halves(jnp.concatenate([xs_ref[j, rows, :] for j in range(ROW_CHUNKS)], axis=1))
            lo = lo.astype(BF16)
            hi = hi.astype(BF16)
            gate = _dot(lo, wg_ref[:PACKED, :]) + _dot(hi, wg_ref[PACKED:, :])
            up = _dot(lo, wu_ref[:PACKED, :]) + _dot(hi, wu_ref[PACKED:, :])
            hmid = jax.nn.silu(gate) * up
            y = _pack_halves(_dot(hmid.astype(BF16), wd_ref[...]))
            for j in range(ROW_CHUNKS):
                y_out[j, rows, :] = y[:, j * LANES:(j + 1) * LANES]

    @pl.when(pl.program_id(0) * EXPERT_BLOCKS_PER_STEP >= nv_ref[0])
    def _():
        y_out[...] = jnp.zeros_like(y_out)


def _experts(block_e, nvalid, xs, wg, wu, wd):
    rows = xs.shape[1]
    nsub = EXPERT_BLOCKS_PER_STEP
    step_rows = MOE_ROWS * nsub
    assert rows % step_rows == 0
    blk = pl.BlockSpec((ROW_CHUNKS, step_rows, LANES), lambda i, be, nv: (0, i, 0))
    w_specs = []
    for sub in range(nsub):
        pick = lambda i, be, nv, sub=sub: (be[i * nsub + sub], 0, 0)
        w_specs += [pl.BlockSpec((None, D_MODEL, D_EXPERT), pick), pl.BlockSpec((None, D_MODEL, D_EXPERT), pick),
                    pl.BlockSpec((None, D_EXPERT, D_MODEL), pick)]
    return pl.pallas_call(
        _experts_kernel,
        grid_spec=pltpu.PrefetchScalarGridSpec(
            num_scalar_prefetch=2,
            grid=(rows // step_rows,),
            in_specs=[blk] + w_specs,
            out_specs=blk,
        ),
        out_shape=jax.ShapeDtypeStruct((ROW_CHUNKS, rows, LANES), U32),
        compiler_params=_cparams("arbitrary"),
        name="experts",
    )(block_e, nvalid, xs, *([wg, wu, wd] * nsub))


def _combine_kernel(base_ref, yg_ref, gate_ref, lg_ref, lb_ref, o_ref):
    gate = gate_ref[...]
    routed = None
    for k in range(TOP_K):
        lo, hi = _unpack_halves(_load_chunks(yg_ref.at[k]))
        yk = jnp.concatenate([lo, hi], axis=1) * gate[:, k:k + 1]
        routed = yk if routed is None else routed + yk
    o_ref[...] = _layer_norm(base_ref[...] + routed, lg_ref[...], lb_ref[...])


def _combine(base, yg, gate_t, lg, lb):
    n = base.shape[0]
    t = TOK_TILE
    tok = pl.BlockSpec((t, D_MODEL), lambda i: (i, 0))
    return pl.pallas_call(
        _combine_kernel,
        grid=(n // t,),
        in_specs=[tok, pl.BlockSpec((TOP_K, ROW_CHUNKS, t, LANES), lambda i: (0, 0, i, 0)),
                  pl.BlockSpec((t, TOP_K), lambda i: (i, 0)), _full((1, D_MODEL)), _full((1, D_MODEL))],
        out_specs=tok,
        out_shape=jax.ShapeDtypeStruct((n, D_MODEL), F32),
        compiler_params=_cparams("parallel"),
        name="combine",
    )(base, yg, gate_t, lg, lb)


def _q_lane_perm():
    order = []
    for c in range(Q_GROUP):
        for g in range(KV_HEADS):
            order.append(g * Q_GROUP + c)
    idx = []
    for h in order:
        idx.extend(range(h * HEAD_DIM, (h + 1) * HEAD_DIM))
    return jnp.asarray(idx, jnp.int32)


def _block_diag(w):
    nb, bi, bo = w.shape
    eye = jnp.eye(nb, dtype=w.dtype)
    return (eye[:, None, :, None] * w[:, :, None, :]).reshape(nb * bi, nb * bo)


def _head_avg(width):
    h = jnp.arange(width) // HEAD_DIM
    return jnp.where(h[:, None] == h[None, :], 1.0 / HEAD_DIM, 0.0).astype(BF16)


def _rope_tables(seq):
    rows = seq // GRID_W
    row_id, col_id = jnp.meshgrid(jnp.arange(rows), jnp.arange(GRID_W), indexing='ij')
    row_id = row_id.reshape(-1).astype(F32)
    col_id = col_id.reshape(-1).astype(F32)
    inv_freq = ROPE_THETA ** (-jnp.arange(ROPE_AXIS_FREQS, dtype=F32) / ROPE_AXIS_FREQS)
    ang = jnp.concatenate([row_id[:, None] * inv_freq, col_id[:, None] * inv_freq], axis=-1)
    cos, sin = jnp.cos(ang), jnp.sin(ang)
    reps = LANES // HEAD_DIM
    return jnp.tile(jnp.concatenate([cos, cos], axis=1), (1, reps)), jnp.tile(jnp.concatenate([-sin, sin], axis=1), (1, reps))


def _prep_layer(p):
    perm = _q_lane_perm()
    w_in = p['w_in']
    q0 = 2 * D_LRU
    w_in = jnp.concatenate([w_in[:, :q0], w_in[:, q0:q0 + D_ATT][:, perm], w_in[:, q0 + D_ATT:]], axis=1)
    w_out = p['w_out']
    row = lambda v: v.reshape(1, -1)
    rw_t = p['router_w'].T
    rwh = rw_t.astype(BF16)
    return dict(
        w_in=w_in.astype(BF16),
        gq=jnp.tile(p['q_norm_g'], ATT_HEADS).reshape(1, D_ATT),
        gk=jnp.tile(p['k_norm_g'], KV_HEADS).reshape(1, KV_DIM),
        conv_w=p['conv_w'], conv_b=row(p['conv_b']),
        w_gates=jnp.stack([jnp.concatenate([_block_diag(p['lru_wa'][d]), _block_diag(p['lru_wx'][d])], axis=1)
                           for d in range(2)]).astype(BF16),
        ba=p['lru_ba'].reshape(2, 1, D_LRU), bx=p['lru_bx'].reshape(2, 1, D_LRU),
        lam=p['lru_lambda'].reshape(2, 1, D_LRU),
        gl=row(p['gn_lru_g']), ga=row(p['gn_att_g'][perm]),
        wo1=w_out[:D_LRU].astype(BF16), wo2=w_out[D_LRU:][perm].astype(BF16),
        l1g=row(p['ln1_g']), l1b=row(p['ln1_b']),
        wq=p['xa_wq'].astype(BF16), wkv=p['xa_wkv'].astype(BF16), wxo=p['xa_wo'].astype(BF16),
        l2g=row(p['ln2_g']), l2b=row(p['ln2_b']),
        rwh=rwh, rwl=(rw_t - rwh.astype(F32)).astype(BF16), rb=p['router_b'].reshape(N_EXPERTS, 1),
        wg=p['w_gate'].astype(BF16), wu=p['w_up'].astype(BF16), wd=p['w_down'].astype(BF16),
        sg=p['sh_gate'].astype(BF16), su=p['sh_up'].astype(BF16), sd=p['sh_down'].astype(BF16),
        l3g=row(p['ln3_g']), l3b=row(p['ln3_b']),
    )


def _moe(x2w, eidx, gate, rank, counts, lp):
    n = eidx.shape[1]
    r = MOE_ROWS
    n_blocks = (n * TOP_K) // r + N_EXPERTS
    rows = n_blocks * r
    counts = counts.reshape(N_EXPERTS)
    pad_counts = ((counts + r - 1) // r) * r
    pad_end = jnp.cumsum(pad_counts)
    pad_start = pad_end - pad_counts
    start_of = jnp.sum(jnp.where(eidx[:, :, None] == jnp.arange(N_EXPERTS), pad_start, 0), axis=-1)
    dest = start_of + rank
    block_e = jnp.minimum(jnp.sum(pad_end[None, :] // r <= jnp.arange(n_blocks)[:, None], axis=1),
                          N_EXPERTS - 1).astype(jnp.int32)
    nvalid = (pad_end[-1] // r).astype(jnp.int32).reshape(1)
    sidx = (dest[:, None, :] + (jnp.arange(ROW_CHUNKS, dtype=jnp.int32) * rows)[None, :, None]).reshape(TOP_K, ROW_CHUNKS * n)
    xs = _sc_dispatch(x2w.reshape(ROW_CHUNKS * n, LANES), sidx, ROW_CHUNKS * rows)
    ys = _experts(block_e, nvalid, xs.reshape(ROW_CHUNKS, rows, LANES), lp['wg'], lp['wu'], lp['wd'])
    yg = _sc_gather(ys.reshape(ROW_CHUNKS * rows, LANES), sidx)
    return yg.reshape(TOP_K, ROW_CHUNKS, n, LANES), gate.T


def _layer(x, mem, lp, alpha, consts, first):
    bsz, seq, _ = x.shape
    n = bsz * seq
    xf = x.reshape(n, D_MODEL)
    cos_t, sin_t = _rope_tables(seq)
    g, xbr, q, k, v = _in_proj(xf, seq, consts['ln_g'], consts['ln_b'], lp['w_in'], lp['gq'], lp['gk'],
                               cos_t, sin_t, consts['avgq'], consts['avgk'])
    h = _lru(xbr.reshape(bsz, seq, D_LRU), lp['conv_w'], lp['conv_b'], lp['w_gates'], lp['ba'], lp['bx'], lp['lam'])
    y_att = _attn(q.reshape(bsz, seq, D_ATT), k.reshape(bsz, seq, KV_DIM), v.reshape(bsz, seq, KV_DIM))
    kmem, vmem = _kv_mem(mem.reshape(bsz * N_MEM, D_MODEL), lp['wkv'])
    x2w, eidx, gate, rank, counts, base = _post(
        xf, g, h.reshape(2, n, D_LRU), y_att.reshape(n, D_ATT), kmem, vmem, seq, alpha,
        consts['ln_g'], consts['ln_b'], lp['gl'], lp['ga'], lp['wo1'], lp['wo2'], lp['l1g'], lp['l1b'],
        lp['wq'], lp['wxo'], lp['l2g'], lp['l2b'],
        lp['rwh'], lp['rwl'], lp['rb'], consts['tri'], lp['sg'], lp['su'], lp['sd'])
    yg, gate_t = _moe(x2w, eidx, gate, rank, counts, lp)
    out = _combine(base, yg, gate_t, lp['l3g'], lp['l3b'])
    return out.reshape(bsz, seq, D_MODEL)


def kernel(x_prompt, x_sample, mem_prompt, mem_sample, ln_in_g, ln_in_b, w_in, conv_w, conv_b, lru_wa, lru_ba, lru_wx, lru_bx, lru_lambda, q_norm_g, k_norm_g, gn_lru_g, gn_att_g, w_out, ln1_g, ln1_b, xa_wq, xa_wkv, xa_wo, ln2_g, ln2_b, router_w, router_b, w_gate, w_up, w_down, sh_gate, sh_up, sh_down, ln3_g, ln3_b):
    depth = w_in.shape[0]
    assert depth == 1, "the fused in_proj/post kernels assume the input LayerNorm feeds a single layer"
    alpha = (2.0 * depth) ** 0.25
    stacked = dict(w_in=w_in, conv_w=conv_w, conv_b=conv_b, lru_wa=lru_wa, lru_ba=lru_ba, lru_wx=lru_wx,
                   lru_bx=lru_bx, lru_lambda=lru_lambda, q_norm_g=q_norm_g, k_norm_g=k_norm_g, gn_lru_g=gn_lru_g,
                   gn_att_g=gn_att_g, w_out=w_out, ln1_g=ln1_g, ln1_b=ln1_b, xa_wq=xa_wq, xa_wkv=xa_wkv,
                   xa_wo=xa_wo, ln2_g=ln2_g, ln2_b=ln2_b, router_w=router_w, router_b=router_b, w_gate=w_gate,
                   w_up=w_up, w_down=w_down, sh_gate=sh_gate, sh_up=sh_up, sh_down=sh_down, ln3_g=ln3_g, ln3_b=ln3_b)
    lp = _prep_layer({name: val[0] for name, val in stacked.items()})
    tri = (jnp.arange(TOK_TILE)[:, None] < jnp.arange(TOK_TILE)[None, :]).astype(BF16)
    consts = dict(ln_g=ln_in_g.reshape(1, D_MODEL), ln_b=ln_in_b.reshape(1, D_MODEL),
                  avgq=_head_avg(min(D_ATT, MXU_DIM)), avgk=_head_avg(min(KV_DIM, MXU_DIM)), tri=tri)
    y_prompt = _layer(x_prompt, mem_prompt, lp, alpha, consts, True)
    y_sample = _layer(x_sample, mem_sample, lp, alpha, consts, False)
    return (y_prompt, y_sample)
```

```python
import functools

import jax
import jax.numpy as jnp
from jax import lax
from jax.experimental import pallas as pl
from jax.experimental.pallas import tpu as pltpu
from jax.experimental.pallas import tpu_sc as plsc

F32 = jnp.float32
BF16 = jnp.bfloat16
U32 = jnp.uint32

D_MODEL = 1024
HEAD_DIM = 64
D_LRU = 512
LRU_BLOCKS = 8
LRU_C = 8.0
CONV_W = 4
CONV_LEFT = 2
ATT_HEADS = 8
KV_HEADS = 2
Q_GROUP = ATT_HEADS // KV_HEADS
D_ATT = ATT_HEADS * HEAD_DIM
KV_DIM = KV_HEADS * HEAD_DIM
D_IN = 2 * D_LRU + D_ATT + 2 * KV_DIM
GRID_W = 64
ROPE_THETA = 10000.0
ROPE_AXIS_FREQS = HEAD_DIM // 4
N_MEM = 256
X_HEADS = 4
X_HEAD_DIM = D_MODEL // X_HEADS
N_EXPERTS = 64
TOP_K = 8
N_GROUPS = 8
GROUP_SIZE = N_EXPERTS // N_GROUPS
TOPK_GROUPS = 4
D_EXPERT = 256
PACKED = D_MODEL // 2
ROUTED_SCALE = 2.5
LN_EPS = 1e-5
RMS_EPS = 1e-6
LOG2_E = 1.4426950408889634

LANES = 128
SUBLANES = 8
MXU_DIM = 256
TOK_TILE = 512
LRU_TILE = 512
ATT_TQ = 256
ATT_TK = 2048
ATT_MAX_INLINE_STEPS = 4
MOE_ROWS = 512
EXPERT_BLOCKS_PER_STEP = 2
COMBINE_PARTS = 2
ROW_CHUNKS = PACKED // LANES
SC_CORES = 2
SC_SUBCORES = 16
SC_WINDOW = 128
VMEM_LIMIT = 48 * 1024 * 1024


def _cparams(*sem):
    return pltpu.CompilerParams(dimension_semantics=sem, vmem_limit_bytes=VMEM_LIMIT)


def _full(shape):
    return pl.BlockSpec(shape, lambda *_: (0,) * len(shape))


def _layer_norm(x, g, b):
    mu = jnp.mean(x, axis=-1, keepdims=True)
    xc = x - mu
    var = jnp.mean(xc * xc, axis=-1, keepdims=True)
    return xc * lax.rsqrt(var + LN_EPS) * g + b


def _row_rms(x, g):
    return x * lax.rsqrt(jnp.mean(x * x, axis=-1, keepdims=True) + RMS_EPS) * g


def _dot(a, b):
    return jnp.dot(a, b, preferred_element_type=F32)


def _dot_nt(a, b):
    return lax.dot_general(a, b, (((1,), (1,)), ((), ())), preferred_element_type=F32)


def _pack_halves(x):
    w = x.shape[1] // 2
    lo = lax.bitcast_convert_type(x[:, :w].astype(BF16).astype(F32), U32)
    hi = lax.bitcast_convert_type(x[:, w:].astype(BF16).astype(F32), U32)
    return (lo >> 16) | (hi & jnp.uint32(0xFFFF0000))


def _unpack_halves(w):
    lo = lax.bitcast_convert_type(w << 16, F32)
    hi = lax.bitcast_convert_type(w & jnp.uint32(0xFFFF0000), F32)
    return lo, hi


def _store_chunks(ref, w):
    for j in range(ROW_CHUNKS):
        ref[j] = w[:, j * LANES:(j + 1) * LANES]


def _load_chunks(ref):
    return jnp.concatenate([ref[j] for j in range(ROW_CHUNKS)], axis=1)


def _split_bf16(x):
    hi = x.astype(BF16)
    lo = (x - hi.astype(F32)).astype(BF16)
    return hi, lo


def _head_mean_square(z, avg_ref):
    hi, lo = _split_bf16(z * z)
    avg = avg_ref[...]
    w = avg.shape[0]
    parts = [_dot(hi[:, c:c + w], avg) + _dot(lo[:, c:c + w], avg) for c in range(0, z.shape[1], w)]
    return parts[0] if len(parts) == 1 else jnp.concatenate(parts, axis=1)


def _rope(x, cos, sin_signed):
    lane = lax.broadcasted_iota(jnp.int32, (1, LANES), 1)
    low_half = (lane % HEAD_DIM) < (HEAD_DIM // 2)
    outs = []
    for c in range(x.shape[1] // LANES):
        xc = x[:, c * LANES:(c + 1) * LANES]
        partner = jnp.where(low_half, pltpu.roll(xc, LANES - HEAD_DIM // 2, 1), pltpu.roll(xc, HEAD_DIM // 2, 1))
        outs.append(xc * cos + partner * sin_signed)
    return outs[0] if len(outs) == 1 else jnp.concatenate(outs, axis=1)


def _in_proj_kernel(x_ref, lg_ref, lb_ref, w_ref, gq_ref, gk_ref, cos_ref, sin_ref, avgq_ref, avgk_ref,
                    g_out, xbr_out, q_out, k_out, v_out):
    xn = _layer_norm(x_ref[...], lg_ref[...], lb_ref[...])
    z = _dot(xn.astype(BF16), w_ref[...])
    g_out[...] = z[:, :D_LRU]
    xbr_out[...] = z[:, D_LRU:2 * D_LRU]
    zq = z[:, 2 * D_LRU:2 * D_LRU + D_ATT]
    zk = z[:, 2 * D_LRU + D_ATT:2 * D_LRU + D_ATT + KV_DIM]
    v_out[...] = z[:, 2 * D_LRU + D_ATT + KV_DIM:].astype(BF16)
    cos = cos_ref[...]
    sin = sin_ref[...]
    q = zq * lax.rsqrt(_head_mean_square(zq, avgq_ref) + RMS_EPS) * gq_ref[...]
    k = zk * lax.rsqrt(_head_mean_square(zk, avgk_ref) + RMS_EPS) * gk_ref[...]
    q_out[...] = (_rope(q, cos, sin) * (HEAD_DIM ** -0.5 * LOG2_E)).astype(BF16)
    k_out[...] = _rope(k, cos, sin).astype(BF16)


def _in_proj(x, seq, ln_g, ln_b, w_in_b, gq, gk, cos_t, sin_t, avgq, avgk):
    n = x.shape[0]
    t = TOK_TILE
    nseq = seq // t
    tok = lambda w: pl.BlockSpec((t, w), lambda i: (i, 0))
    pos = pl.BlockSpec((t, LANES), lambda i: (i % nseq, 0))
    return pl.pallas_call(
        _in_proj_kernel,
        grid=(n // t,),
        in_specs=[tok(D_MODEL), _full((1, D_MODEL)), _full((1, D_MODEL)), _full((D_MODEL, D_IN)),
                  _full((1, D_ATT)), _full((1, KV_DIM)), pos, pos, _full(avgq.shape), _full(avgk.shape)],
        out_specs=[tok(D_LRU), tok(D_LRU), tok(D_ATT), tok(KV_DIM), tok(KV_DIM)],
        out_shape=[jax.ShapeDtypeStruct((n, D_LRU), F32), jax.ShapeDtypeStruct((n, D_LRU), F32),
                   jax.ShapeDtypeStruct((n, D_ATT), BF16), jax.ShapeDtypeStruct((n, KV_DIM), BF16),
                   jax.ShapeDtypeStruct((n, KV_DIM), BF16)],
        compiler_params=_cparams("parallel"),
        name="in_proj",
    )(x, ln_g, ln_b, w_in_b, gq, gk, cos_t, sin_t, avgq, avgk)


def _softplus(x):
    return jnp.maximum(x, 0.0) + jnp.log1p(jnp.exp(-jnp.abs(x)))


def _lru_kernel(xp_ref, x_ref, xn_ref, cw_ref, cb_ref, w_ref, ba_ref, bx_ref, lam_ref, h_ref,
                a_s, b_s, c_s, *, tb, nt):
    d = pl.program_id(1)
    i = pl.program_id(2)
    tbi = jnp.where(d == 0, i, nt - 1 - i)
    x = x_ref[...]
    prev = jnp.where(tbi > 0, xp_ref[...], 0.0)
    nxt = jnp.where(tbi < nt - 1, xn_ref[...], 0.0)
    xe = jnp.concatenate([prev, x, nxt], axis=0)
    cw = cw_ref[...]
    o = SUBLANES - CONV_LEFT
    xc = xe[o:o + tb] * cw[0:1]
    for k in range(1, CONV_W):
        xc = xc + xe[o + k:o + k + tb] * cw[k:k + 1]
    xc = xc + cb_ref[...]
    rg = _dot(xc.astype(BF16), w_ref[...])
    r = jax.nn.sigmoid(rg[:, :D_LRU] + ba_ref[...])
    ig = jax.nn.sigmoid(rg[:, D_LRU:] + bx_ref[...])
    log_a = -LRU_C * r * _softplus(-lam_ref[...])
    a = jnp.exp(log_a)
    b = jnp.sqrt(-jnp.tanh(log_a) * (a * a + 1.0)) * (ig * xc)

    ngroups = tb // SUBLANES
    row = lax.broadcasted_iota(jnp.int32, (ngroups, SUBLANES, D_LRU), 1)

    @pl.when(i == 0)
    def _():
        c_s[...] = jnp.zeros_like(c_s)

    def scan(reverse):
        aa = a.reshape(ngroups, SUBLANES, D_LRU)
        bb = b.reshape(ngroups, SUBLANES, D_LRU)
        for k in (1, 2, 4):
            if reverse:
                valid = row < SUBLANES - k
                shift = SUBLANES - k
            else:
                valid = row >= k
                shift = k
            a_sh = jnp.where(valid, pltpu.roll(aa, shift, 1), 1.0)
            b_sh = jnp.where(valid, pltpu.roll(bb, shift, 1), 0.0)
            bb = aa * b_sh + bb
            aa = aa * a_sh
        a_s[...] = aa.reshape(tb, D_LRU)
        b_s[...] = bb.reshape(tb, D_LRU)

        def body(gi, c):
            g = (ngroups - 1 - gi) if reverse else gi
            sl = pl.ds(pl.multiple_of(g * SUBLANES, SUBLANES), SUBLANES)
            h = a_s[sl, :] * c + b_s[sl, :]
            h_ref[sl, :] = h
            edge = h[0:1, :] if reverse else h[SUBLANES - 1:SUBLANES, :]
            return jnp.broadcast_to(edge, (SUBLANES, D_LRU))

        c_s[...] = lax.fori_loop(0, ngroups, body, c_s[...], unroll=8)

    @pl.when(d == 0)
    def _():
        scan(False)

    @pl.when(d == 1)
    def _():
        scan(True)


def _lru(xbr, conv_w, conv_b, w_gates, ba, bx, lam):
    bsz, seq, _ = xbr.shape
    tb = LRU_TILE
    nt = seq // tb
    r8 = tb // SUBLANES
    nb8 = seq // SUBLANES

    def tblk(d, i):
        return jnp.where(d == 0, i, nt - 1 - i)

    cur = pl.BlockSpec((None, tb, D_LRU), lambda b, d, i: (b, tblk(d, i), 0))
    prv = pl.BlockSpec((None, SUBLANES, D_LRU), lambda b, d, i: (b, jnp.maximum(tblk(d, i) * r8 - 1, 0), 0))
    nxt = pl.BlockSpec((None, SUBLANES, D_LRU), lambda b, d, i: (b, jnp.minimum((tblk(d, i) + 1) * r8, nb8 - 1), 0))
    per_dir = lambda shape: pl.BlockSpec((None,) + shape, lambda b, d, i: (d,) + (0,) * len(shape))
    return pl.pallas_call(
        functools.partial(_lru_kernel, tb=tb, nt=nt),
        grid=(bsz, 2, nt),
        in_specs=[prv, cur, nxt, _full((CONV_W, D_LRU)), _full((1, D_LRU)),
                  per_dir((D_LRU, 2 * D_LRU)), per_dir((1, D_LRU)), per_dir((1, D_LRU)), per_dir((1, D_LRU))],
        out_specs=pl.BlockSpec((None, None, tb, D_LRU), lambda b, d, i: (d, b, tblk(d, i), 0)),
        out_shape=jax.ShapeDtypeStruct((2, bsz, seq, D_LRU), F32),
        scratch_shapes=[pltpu.VMEM((tb, D_LRU), F32), pltpu.VMEM((tb, D_LRU), F32),
                        pltpu.VMEM((SUBLANES, D_LRU), F32)],
        compiler_params=_cparams("parallel", "arbitrary", "arbitrary"),
        name="lru",
    )(xbr, xbr, xbr, conv_w, conv_b, w_gates, ba, bx, lam)


def _attn_kernel(q_ref, k_ref, v_ref, o_ref, qs_s, m_s, l_s, acc_s, *, tq, tk, seq):
    lane = lax.broadcasted_iota(jnp.int32, (1, LANES), 1)
    low = lane < HEAD_DIM
    nchunk = D_ATT // LANES
    for c in range(nchunk):
        qc = q_ref[:, c * LANES:(c + 1) * LANES]
        zero = jnp.zeros_like(qc)
        qs_s[c, :tq, :] = jnp.where(low, qc, zero)
        qs_s[c, tq:, :] = jnp.where(low, zero, qc)
    def step(j, first):
        sl = pl.ds(j * tk, tk) if isinstance(j, int) else pl.ds(pl.multiple_of(j * tk, tk), tk)
        kj = k_ref[sl, :]
        vj = v_ref[sl, :]
        for c in range(nchunk):
            s = _dot_nt(qs_s[c], kj)
            row_max = jnp.max(s, axis=1, keepdims=True)
            if first:
                m_next = jnp.broadcast_to(row_max, (2 * tq, LANES))
            else:
                m_prev = m_s[c]
                m_next = jnp.maximum(m_prev, row_max)
                alpha = jnp.exp2(m_prev - m_next)
            ps = [jnp.exp2(s[:, t * LANES:(t + 1) * LANES] - m_next) for t in range(tk // LANES)]
            part = ps[0]
            for pt in ps[1:]:
                part = part + pt
            pv = _dot(jnp.concatenate(ps, axis=1).astype(BF16), vj)
            if first:
                l_s[c] = part
                acc_s[c] = pv
            else:
                l_s[c] = alpha * l_s[c] + part
                acc_s[c] = alpha * acc_s[c] + pv
            m_s[c] = m_next

    step(0, True)
    nk = seq // tk
    if nk <= ATT_MAX_INLINE_STEPS:
        for j in range(1, nk):
            step(j, False)
    else:
        def body(j, carry):
            step(j, False)
            return carry

        lax.fori_loop(1, nk, body, 0)
    for c in range(nchunk):
        o = acc_s[c] / jnp.sum(l_s[c], axis=1, keepdims=True)
        o_ref[:, c * LANES:(c + 1) * LANES] = jnp.where(low, o[:tq], o[tq:])


def _attn(q, k, v):
    bsz, seq, _ = q.shape
    tq, tk = ATT_TQ, min(ATT_TK, seq)
    assert seq % tq == 0 and seq % tk == 0
    kv = pl.BlockSpec((None, seq, KV_DIM), lambda b, i: (b, 0, 0))
    qo = pl.BlockSpec((None, tq, D_ATT), lambda b, i: (b, i, 0))
    stat = pltpu.VMEM((D_ATT // LANES, 2 * tq, LANES), F32)
    return pl.pallas_call(
        functools.partial(_attn_kernel, tq=tq, tk=tk, seq=seq),
        grid=(bsz, seq // tq),
        in_specs=[qo, kv, kv],
        out_specs=qo,
        out_shape=jax.ShapeDtypeStruct((bsz, seq, D_ATT), F32),
        scratch_shapes=[pltpu.VMEM((D_ATT // LANES, 2 * tq, LANES), BF16), stat, stat, stat],
        compiler_params=_cparams("parallel", "parallel"),
        name="attn",
    )(q, k, v)


def _kv_mem_kernel(m_ref, w_ref, k_out, v_out):
    kv = _dot(m_ref[...].astype(BF16), w_ref[...])
    k_out[...] = kv[:, :D_MODEL].astype(BF16)
    v_out[...] = kv[:, D_MODEL:].astype(BF16)


def _kv_mem(mem, wkv_b):
    rows = mem.shape[0]
    blk = pl.BlockSpec((N_MEM, D_MODEL), lambda i: (i, 0))
    return pl.pallas_call(
        _kv_mem_kernel,
        grid=(rows // N_MEM,),
        in_specs=[blk, _full((D_MODEL, 2 * D_MODEL))],
        out_specs=[blk, blk],
        out_shape=[jax.ShapeDtypeStruct((rows, D_MODEL), BF16)] * 2,
        compiler_params=_cparams("parallel"),
        name="kv_mem",
    )(mem, wkv_b)


def _post_kernel(x_ref, g_ref, hf_ref, hb_ref, ya_ref, km_ref, vm_ref,
                 lg_ref, lb_ref, gl_ref, ga_ref, wo1_ref, wo2_ref, l1g_ref, l1b_ref,
                 wq_ref, wxo_ref, l2g_ref, l2b_ref, rwh_ref, rwl_ref, rb_ref, tri_ref, sg_ref, su_ref, sd_ref,
                 x2w_out, eidx_out, gate_out, rank_out, cnt_out, base_out, cnt_s, *, alpha):
    x0 = _layer_norm(x_ref[...], lg_ref[...], lb_ref[...])
    y_lru = (hf_ref[...] + hb_ref[...]) * jax.nn.gelu(g_ref[...], approximate=True)
    y_lru = _row_rms(y_lru, gl_ref[...])
    y_att = _row_rms(ya_ref[...], ga_ref[...])
    y = _dot(y_lru.astype(BF16), wo1_ref[...]) + _dot(y_att.astype(BF16), wo2_ref[...])
    x1 = _layer_norm(alpha * x0 + y, l1g_ref[...], l1b_ref[...])
    xq = _dot(x1.astype(BF16), wq_ref[...]).astype(BF16)
    heads = []
    for h in range(X_HEADS):
        sl = slice(h * X_HEAD_DIM, (h + 1) * X_HEAD_DIM)
        s = _dot_nt(xq[:, sl], km_ref[:, sl]) * (X_HEAD_DIM ** -0.5)
        e = jnp.exp(s - jnp.max(s, axis=1, keepdims=True))
        o = _dot(e.astype(BF16), vm_ref[:, sl]) / jnp.sum(e, axis=1, keepdims=True)
        heads.append(o.astype(BF16))
    o = jnp.concatenate(heads, axis=1)
    x2 = _layer_norm(alpha * x1 + _dot(o, wxo_ref[...]), l2g_ref[...], l2b_ref[...])
    _store_chunks(x2w_out, _pack_halves(x2))
    _route(x2, rwh_ref, rwl_ref, rb_ref, tri_ref, sg_ref, su_ref, sd_ref,
           eidx_out, gate_out, rank_out, cnt_out, base_out, cnt_s, alpha)


def _post(x, g, h, y_att, kmem, vmem, seq, alpha, ln_g, ln_b, gl, ga, wo1, wo2, l1g, l1b, wq, wxo, l2g, l2b,
          rwh, rwl, rb, tri, sg, su, sd):
    n = x.shape[0]
    t = TOK_TILE
    nseq = seq // t
    tok = lambda w: pl.BlockSpec((t, w), lambda i: (i, 0))
    hdir = lambda d: pl.BlockSpec((None, t, D_LRU), lambda i: (d, i, 0))
    mem = pl.BlockSpec((N_MEM, D_MODEL), lambda i: (i // nseq, 0))
    vec = _full((1, D_MODEL))
    half = _full((1, D_LRU))
    sq = _full((D_MODEL, D_MODEL))
    kt = pl.BlockSpec((TOP_K, t), lambda i: (0, i))
    return pl.pallas_call(
        functools.partial(_post_kernel, alpha=alpha),
        grid=(n // t,),
        in_specs=[tok(D_MODEL), tok(D_LRU), hdir(0), hdir(1), tok(D_ATT), mem, mem,
                  vec, vec, half, half, _full((D_LRU, D_MODEL)), _full((D_ATT, D_MODEL)), vec, vec,
                  sq, sq, vec, vec,
                  _full((N_EXPERTS, D_MODEL)), _full((N_EXPERTS, D_MODEL)), _full((N_EXPERTS, 1)), _full((t, t)),
                  _full((D_MODEL, D_EXPERT)), _full((D_MODEL, D_EXPERT)), _full((D_EXPERT, D_MODEL))],
        out_specs=[pl.BlockSpec((ROW_CHUNKS, t, LANES), lambda i: (0, i, 0)), kt, kt, kt,
                   _full((N_EXPERTS, 1)), tok(D_MODEL)],
        out_shape=[jax.ShapeDtypeStruct((ROW_CHUNKS, n, LANES), U32),
                   jax.ShapeDtypeStruct((TOP_K, n), jnp.int32), jax.ShapeDtypeStruct((TOP_K, n), F32),
                   jax.ShapeDtypeStruct((TOP_K, n), jnp.int32), jax.ShapeDtypeStruct((N_EXPERTS, 1), jnp.int32),
                   jax.ShapeDtypeStruct((n, D_MODEL), F32)],
        scratch_shapes=[pltpu.VMEM((N_EXPERTS, 1), F32)],
        compiler_params=_cparams("arbitrary"),
        name="post",
    )(x, g, h, h, y_att, kmem, vmem, ln_g, ln_b, gl, ga, wo1, wo2, l1g, l1b, wq, wxo, l2g, l2b,
      rwh, rwl, rb, tri, sg, su, sd)


def _first_max(v, n):
    rows = lax.broadcasted_iota(jnp.int32, v.shape, 0)
    m = jnp.max(v, axis=0, keepdims=True)
    idx = jnp.min(jnp.where(v == m, rows, n), axis=0, keepdims=True)
    return m, idx, rows == idx


def _route(x, rwh_ref, rwl_ref, rb_ref, tri_ref, sg_ref, su_ref, sd_ref,
           eidx_out, gate_out, rank_out, cnt_out, base_out, cnt_s, alpha):
    @pl.when(pl.program_id(0) == 0)
    def _():
        cnt_s[...] = jnp.zeros_like(cnt_s)

    xh, xl = _split_bf16(x)
    logits = _dot_nt(rwh_ref[...], xh) + _dot_nt(rwh_ref[...], xl) + _dot_nt(rwl_ref[...], xh)
    scores = jax.nn.sigmoid(logits)
    biased = scores + rb_ref[...]
    t = x.shape[0]
    neg = -jnp.inf

    gscore = []
    for g in range(N_GROUPS):
        vg = biased[g * GROUP_SIZE:(g + 1) * GROUP_SIZE, :]
        m1, _, hit = _first_max(vg, GROUP_SIZE)
        m2 = jnp.max(jnp.where(hit, neg, vg), axis=0, keepdims=True)
        gscore.append(m1 + m2)
    gs = jnp.concatenate(gscore, axis=0)
    gsel = jnp.zeros(gs.shape, jnp.bool_)
    for _ in range(TOPK_GROUPS):
        _, _, hit = _first_max(gs, N_GROUPS)
        gsel = jnp.logical_or(gsel, hit)
        gs = jnp.where(hit, neg, gs)
    emask = jnp.concatenate(
        [jnp.broadcast_to(gsel[g:g + 1, :], (GROUP_SIZE, t)) for g in range(N_GROUPS)], axis=0)
    masked = jnp.where(emask, biased, neg)

    idxs, gates, hits = [], [], []
    sel = jnp.zeros(masked.shape, jnp.bool_)
    for _ in range(TOP_K):
        _, idx, hit = _first_max(masked, N_EXPERTS)
        idxs.append(idx)
        hits.append(hit)
        gates.append(jnp.sum(jnp.where(hit, scores, 0.0), axis=0, keepdims=True))
        sel = jnp.logical_or(sel, hit)
        masked = jnp.where(hit, neg, masked)
    gsum = gates[0]
    for gk in gates[1:]:
        gsum = gsum + gk
    gate = jnp.concatenate(gates, axis=0) / gsum * ROUTED_SCALE

    chosen = jnp.where(sel, 1.0, 0.0)
    before = _dot(chosen.astype(BF16), tri_ref[...]) + cnt_s[...]
    ranks = [jnp.sum(jnp.where(hit, before, 0.0), axis=0, keepdims=True) for hit in hits]
    cnt = cnt_s[...] + jnp.sum(chosen, axis=1, keepdims=True)
    cnt_s[...] = cnt

    eidx_out[...] = jnp.concatenate(idxs, axis=0)
    gate_out[...] = gate
    rank_out[...] = jnp.concatenate(ranks, axis=0).astype(jnp.int32)
    cnt_out[...] = cnt.astype(jnp.int32)

    hmid = jax.nn.silu(_dot(xh, sg_ref[...])) * _dot(xh, su_ref[...])
    base_out[...] = alpha * x + _dot(hmid.astype(BF16), sd_ref[...])


def _sc_mesh():
    return plsc.VectorSubcoreMesh(core_axis_name="core", subcore_axis_name="subcore")


def _sc_worker_base(per_worker):
    return (lax.axis_index("subcore") * SC_CORES + lax.axis_index("core")) * per_worker


def _sc_dispatch(x_sub, sidx, rows_out):
    nsub = x_sub.shape[0]
    win = SC_WINDOW
    per_worker = nsub // (SC_CORES * SC_SUBCORES)
    assert per_worker * SC_CORES * SC_SUBCORES == nsub and per_worker % win == 0

    nwin = per_worker // win
    assert nwin % 2 == 0

    def body(x_hbm, idx_hbm, out_hbm, idx_v, rows_v, lsem0, lsem1, ssem0, ssem1):
        base = _sc_worker_base(per_worker)
        lsem = (lsem0, lsem1)
        ssem = (ssem0, ssem1)

        def loads(ci, b):
            off = base + ci * win
            return (pltpu.make_async_copy(idx_hbm.at[:, pl.ds(off, win)], idx_v.at[b], lsem[b]),
                    pltpu.make_async_copy(x_hbm.at[pl.ds(off, win)], rows_v.at[b], lsem[b]))

        def scatters(b):
            return [pltpu.make_async_copy(rows_v.at[b], out_hbm.at[idx_v.at[b, k]], ssem[b]) for k in range(TOP_K)]

        for cp in loads(0, 0):
            cp.start()

        @pl.loop(0, nwin, step=2)
        def _(c0):
            for b in range(2):
                ci = c0 + b
                for cp in loads(ci, b):
                    cp.wait()
                for cp in scatters(b):
                    cp.start()

                @pl.when(ci >= 1)
                def _():
                    for cp in scatters(1 - b):
                        cp.wait()

                @pl.when(ci + 1 < nwin)
                def _():
                    for cp in loads(ci + 1, 1 - b):
                        cp.start()

        for cp in scatters(1):
            cp.wait()

    return pl.kernel(
        body, mesh=_sc_mesh(), out_type=jax.ShapeDtypeStruct((rows_out, LANES), U32),
        scratch_types=[pltpu.VMEM((2, TOP_K, win), jnp.int32), pltpu.VMEM((2, win, LANES), U32)]
        + [pltpu.SemaphoreType.DMA] * 4,
    )(x_sub, sidx)


def _sc_gather(y_sub, sidx):
    nsub = sidx.shape[1]
    win = SC_WINDOW
    pair = 2
    npairs = TOP_K // pair
    per_worker = nsub // (SC_CORES * SC_SUBCORES)
    assert per_worker * SC_CORES * SC_SUBCORES == nsub and per_worker % win == 0

    def body(y_hbm, idx_hbm, out_hbm, idx_v, rows_v, gsem, osem0, osem1):
        base = _sc_worker_base(per_worker)
        osem = (osem0, osem1)

        def gathers(q, b):
            return [pltpu.make_async_copy(y_hbm.at[idx_v.at[q * pair + j]], rows_v.at[b, j], gsem) for j in range(pair)]

        def stores(off, q, b):
            return [pltpu.make_async_copy(rows_v.at[b, j], out_hbm.at[q * pair + j, pl.ds(off, win)], osem[b])
                    for j in range(pair)]

        @pl.loop(0, per_worker // win)
        def _(ci):
            off = base + ci * win
            pltpu.sync_copy(idx_hbm.at[:, pl.ds(off, win)], idx_v)
            for q in range(npairs):
                b = q % 2
                if q >= 2:
                    for cp in stores(off, q - 2, b):
                        cp.wait()
                copies = gathers(q, b)
                for cp in copies:
                    cp.start()
                for cp in copies:
                    cp.wait()
                for cp in stores(off, q, b):
                    cp.start()
            for q in (npairs - 2, npairs - 1):
                for cp in stores(off, q, q % 2):
                    cp.wait()

    return pl.kernel(
        body, mesh=_sc_mesh(), out_type=jax.ShapeDtypeStruct((TOP_K, nsub, LANES), U32),
        scratch_types=[pltpu.VMEM((TOP_K, win), jnp.int32), pltpu.VMEM((2, pair, win, LANES), U32)]
        + [pltpu.SemaphoreType.DMA] * 3,
    )(y_sub, sidx)


def _experts_kernel(be_ref, nv_ref, xs_ref, *refs):
    weights, y_out = refs[:-1], refs[-1]
    r = MOE_ROWS

    @pl.when(pl.program_id(0) * EXPERT_BLOCKS_PER_STEP < nv_ref[0])
    def _():
        for sub in range(EXPERT_BLOCKS_PER_STEP):
            wg_ref, wu_ref, wd_ref = weights[3 * sub:3 * sub + 3]
            rows = slice(sub * r, (sub + 1) * r)
            lo, hi = _unpack_halves(jnp.concatenate([xs_ref[j, rows, :] for j in range(ROW_CHUNKS)], axis=1))
            lo = lo.astype(BF16)
            hi = hi.astype(BF16)
            gate = _dot(lo, wg_ref[:PACKED, :]) + _dot(hi, wg_ref[PACKED:, :])
            up = _dot(lo, wu_ref[:PACKED, :]) + _dot(hi, wu_ref[PACKED:, :])
            hmid = jax.nn.silu(gate) * up
            y = _pack_halves(_dot(hmid.astype(BF16), wd_ref[...]))
            for j in range(ROW_CHUNKS):
                y_out[j, rows, :] = y[:, j * LANES:(j + 1) * LANES]

    @pl.when(pl.program_id(0) * EXPERT_BLOCKS_PER_STEP >= nv_ref[0])
    def _():
        y_out[...] = jnp.zeros_like(y_out)


def _experts(block_e, nvalid, xs, wg, wu, wd):
    rows = xs.shape[1]
    nsub = EXPERT_BLOCKS_PER_STEP
    step_rows = MOE_ROWS * nsub
    assert rows % step_rows == 0
    blk = pl.BlockSpec((ROW_CHUNKS, step_rows, LANES), lambda i, be, nv: (0, i, 0))
    w_specs = []
    for sub in range(nsub):
        pick = lambda i, be, nv, sub=sub: (be[i * nsub + sub], 0, 0)
        w_specs += [pl.BlockSpec((None, D_MODEL, D_EXPERT), pick), pl.BlockSpec((None, D_MODEL, D_EXPERT), pick),
                    pl.BlockSpec((None, D_EXPERT, D_MODEL), pick)]
    return pl.pallas_call(
        _experts_kernel,
        grid_spec=pltpu.PrefetchScalarGridSpec(
            num_scalar_prefetch=2,
            grid=(rows // step_rows,),
            in_specs=[blk] + w_specs,
            out_specs=blk,
        ),
        out_shape=jax.ShapeDtypeStruct((ROW_CHUNKS, rows, LANES), U32),
        compiler_params=_cparams("arbitrary"),
        name="experts",
    )(block_e, nvalid, xs, *([wg, wu, wd] * nsub))


def _combine_kernel(base_ref, yg_ref, gate_ref, lg_ref, lb_ref, *refs):
    o_ref = refs[-1]
    gate = gate_ref[...]
    routed = None
    for k in range(TOP_K):
        lo, hi = _unpack_halves(_load_chunks(yg_ref.at[k]))
        yk = jnp.concatenate([lo, hi], axis=1) * gate[:, k:k + 1]
        routed = yk if routed is None else routed + yk
    o_ref[...] = _layer_norm(base_ref[...] + routed, lg_ref[...], lb_ref[...])


def _combine(base, yg, gate_t, lg, lb, prev, part):
    n = base.shape[0]
    t = TOK_TILE
    steps = yg.shape[2] // t
    first = part * steps
    tok = pl.BlockSpec((t, D_MODEL), lambda i: (i + first, 0))
    in_specs = [tok, pl.BlockSpec((TOP_K, ROW_CHUNKS, t, LANES), lambda i: (0, 0, i, 0)),
                pl.BlockSpec((t, TOP_K), lambda i: (i + first, 0)), _full((1, D_MODEL)), _full((1, D_MODEL))]
    args = [base, yg, gate_t, lg, lb]
    aliases = {}
    if prev is not None:
        in_specs.append(pl.BlockSpec(memory_space=pl.ANY))
        args.append(prev)
        aliases = {len(args) - 1: 0}
    return pl.pallas_call(
        _combine_kernel,
        grid=(steps,),
        in_specs=in_specs,
        out_specs=tok,
        out_shape=jax.ShapeDtypeStruct((n, D_MODEL), F32),
        input_output_aliases=aliases,
        compiler_params=_cparams("parallel"),
        name="combine",
    )(*args)


def _q_lane_perm():
    order = []
    for c in range(Q_GROUP):
        for g in range(KV_HEADS):
            order.append(g * Q_GROUP + c)
    idx = []
    for h in order:
        idx.extend(range(h * HEAD_DIM, (h + 1) * HEAD_DIM))
    return jnp.asarray(idx, jnp.int32)


def _block_diag(w):
    nb, bi, bo = w.shape
    eye = jnp.eye(nb, dtype=w.dtype)
    return (eye[:, None, :, None] * w[:, :, None, :]).reshape(nb * bi, nb * bo)


def _head_avg(width):
    h = jnp.arange(width) // HEAD_DIM
    return jnp.where(h[:, None] == h[None, :], 1.0 / HEAD_DIM, 0.0).astype(BF16)


def _rope_tables(seq):
    rows = seq // GRID_W
    row_id, col_id = jnp.meshgrid(jnp.arange(rows), jnp.arange(GRID_W), indexing='ij')
    row_id = row_id.reshape(-1).astype(F32)
    col_id = col_id.reshape(-1).astype(F32)
    inv_freq = ROPE_THETA ** (-jnp.arange(ROPE_AXIS_FREQS, dtype=F32) / ROPE_AXIS_FREQS)
    ang = jnp.concatenate([row_id[:, None] * inv_freq, col_id[:, None] * inv_freq], axis=-1)
    cos, sin = jnp.cos(ang), jnp.sin(ang)
    reps = LANES // HEAD_DIM
    return jnp.tile(jnp.concatenate([cos, cos], axis=1), (1, reps)), jnp.tile(jnp.concatenate([-sin, sin], axis=1), (1, reps))


def _prep_layer(p):
    perm = _q_lane_perm()
    w_in = p['w_in']
    q0 = 2 * D_LRU
    w_in = jnp.concatenate([w_in[:, :q0], w_in[:, q0:q0 + D_ATT][:, perm], w_in[:, q0 + D_ATT:]], axis=1)
    w_out = p['w_out']
    row = lambda v: v.reshape(1, -1)
    rw_t = p['router_w'].T
    rwh = rw_t.astype(BF16)
    return dict(
        w_in=w_in.astype(BF16),
        gq=jnp.tile(p['q_norm_g'], ATT_HEADS).reshape(1, D_ATT),
        gk=jnp.tile(p['k_norm_g'], KV_HEADS).reshape(1, KV_DIM),
        conv_w=p['conv_w'], conv_b=row(p['conv_b']),
        w_gates=jnp.stack([jnp.concatenate([_block_diag(p['lru_wa'][d]), _block_diag(p['lru_wx'][d])], axis=1)
                           for d in range(2)]).astype(BF16),
        ba=p['lru_ba'].reshape(2, 1, D_LRU), bx=p['lru_bx'].reshape(2, 1, D_LRU),
        lam=p['lru_lambda'].reshape(2, 1, D_LRU),
        gl=row(p['gn_lru_g']), ga=row(p['gn_att_g'][perm]),
        wo1=w_out[:D_LRU].astype(BF16), wo2=w_out[D_LRU:][perm].astype(BF16),
        l1g=row(p['ln1_g']), l1b=row(p['ln1_b']),
        wq=p['xa_wq'].astype(BF16), wkv=p['xa_wkv'].astype(BF16), wxo=p['xa_wo'].astype(BF16),
        l2g=row(p['ln2_g']), l2b=row(p['ln2_b']),
        rwh=rwh, rwl=(rw_t - rwh.astype(F32)).astype(BF16), rb=p['router_b'].reshape(N_EXPERTS, 1),
        wg=p['w_gate'].astype(BF16), wu=p['w_up'].astype(BF16), wd=p['w_down'].astype(BF16),
        sg=p['sh_gate'].astype(BF16), su=p['sh_up'].astype(BF16), sd=p['sh_down'].astype(BF16),
        l3g=row(p['ln3_g']), l3b=row(p['ln3_b']),
    )


def _moe(x2w, eidx, gate, rank, counts, lp):
    n = eidx.shape[1]
    r = MOE_ROWS
    n_blocks = (n * TOP_K) // r + N_EXPERTS
    rows = n_blocks * r
    counts = counts.reshape(N_EXPERTS)
    pad_counts = ((counts + r - 1) // r) * r
    pad_end = jnp.cumsum(pad_counts)
    pad_start = pad_end - pad_counts
    start_of = jnp.sum(jnp.where(eidx[:, :, None] == jnp.arange(N_EXPERTS), pad_start, 0), axis=-1)
    dest = start_of + rank
    block_e = jnp.minimum(jnp.sum(pad_end[None, :] // r <= jnp.arange(n_blocks)[:, None], axis=1),
                          N_EXPERTS - 1).astype(jnp.int32)
    nvalid = (pad_end[-1] // r).astype(jnp.int32).reshape(1)
    def sub_rows(d):
        return (d[:, None, :] + (jnp.arange(ROW_CHUNKS, dtype=jnp.int32) * rows)[None, :, None]).reshape(TOP_K, -1)

    xs = _sc_dispatch(x2w.reshape(ROW_CHUNKS * n, LANES), sub_rows(dest), ROW_CHUNKS * rows)
    ys = _experts(block_e, nvalid, xs.reshape(ROW_CHUNKS, rows, LANES), lp['wg'], lp['wu'], lp['wd'])
    ys = ys.reshape(ROW_CHUNKS * rows, LANES)
    part = n // COMBINE_PARTS
    ygs = [_sc_gather(ys, sub_rows(dest[:, p * part:(p + 1) * part])).reshape(TOP_K, ROW_CHUNKS, part, LANES)
           for p in range(COMBINE_PARTS)]
    return ygs, gate.T


def _layer(x, mem, lp, alpha, consts, first):
    bsz, seq, _ = x.shape
    n = bsz * seq
    xf = x.reshape(n, D_MODEL)
    cos_t, sin_t = _rope_tables(seq)
    g, xbr, q, k, v = _in_proj(xf, seq, consts['ln_g'], consts['ln_b'], lp['w_in'], lp['gq'], lp['gk'],
                               cos_t, sin_t, consts['avgq'], consts['avgk'])
    h = _lru(xbr.reshape(bsz, seq, D_LRU), lp['conv_w'], lp['conv_b'], lp['w_gates'], lp['ba'], lp['bx'], lp['lam'])
    y_att = _attn(q.reshape(bsz, seq, D_ATT), k.reshape(bsz, seq, KV_DIM), v.reshape(bsz, seq, KV_DIM))
    kmem, vmem = _kv_mem(mem.reshape(bsz * N_MEM, D_MODEL), lp['wkv'])
    x2w, eidx, gate, rank, counts, base = _post(
        xf, g, h.reshape(2, n, D_LRU), y_att.reshape(n, D_ATT), kmem, vmem, seq, alpha,
        consts['ln_g'], consts['ln_b'], lp['gl'], lp['ga'], lp['wo1'], lp['wo2'], lp['l1g'], lp['l1b'],
        lp['wq'], lp['wxo'], lp['l2g'], lp['l2b'],
        lp['rwh'], lp['rwl'], lp['rb'], consts['tri'], lp['sg'], lp['su'], lp['sd'])
    ygs, gate_t = _moe(x2w, eidx, gate, rank, counts, lp)
    out = None
    for p, yg in enumerate(ygs):
        out = _combine(base, yg, gate_t, lp['l3g'], lp['l3b'], out, p)
    return out.reshape(bsz, seq, D_MODEL)


def kernel(x_prompt, x_sample, mem_prompt, mem_sample, ln_in_g, ln_in_b, w_in, conv_w, conv_b, lru_wa, lru_ba, lru_wx, lru_bx, lru_lambda, q_norm_g, k_norm_g, gn_lru_g, gn_att_g, w_out, ln1_g, ln1_b, xa_wq, xa_wkv, xa_wo, ln2_g, ln2_b, router_w, router_b, w_gate, w_up, w_down, sh_gate, sh_up, sh_down, ln3_g, ln3_b):
    depth = w_in.shape[0]
    assert depth == 1, "the fused in_proj/post kernels assume the input LayerNorm feeds a single layer"
    alpha = (2.0 * depth) ** 0.25
    stacked = dict(w_in=w_in, conv_w=conv_w, conv_b=conv_b, lru_wa=lru_wa, lru_ba=lru_ba, lru_wx=lru_wx,
                   lru_bx=lru_bx, lru_lambda=lru_lambda, q_norm_g=q_norm_g, k_norm_g=k_norm_g, gn_lru_g=gn_lru_g,
                   gn_att_g=gn_att_g, w_out=w_out, ln1_g=ln1_g, ln1_b=ln1_b, xa_wq=xa_wq, xa_wkv=xa_wkv,
                   xa_wo=xa_wo, ln2_g=ln2_g, ln2_b=ln2_b, router_w=router_w, router_b=router_b, w_gate=w_gate,
                   w_up=w_up, w_down=w_down, sh_gate=sh_gate, sh_up=sh_up, sh_down=sh_down, ln3_g=ln3_g, ln3_b=ln3_b)
    lp = _prep_layer({name: val[0] for name, val in stacked.items()})
    tri = (jnp.arange(TOK_TILE)[:, None] < jnp.arange(TOK_TILE)[None, :]).astype(BF16)
    consts = dict(ln_g=ln_in_g.reshape(1, D_MODEL), ln_b=ln_in_b.reshape(1, D_MODEL),
                  avgq=_head_avg(min(D_ATT, MXU_DIM)), avgk=_head_avg(min(KV_DIM, MXU_DIM)), tri=tri)
    y_prompt = _layer(x_prompt, mem_prompt, lp, alpha, consts, True)
    y_sample = _layer(x_sample, mem_sample, lp, alpha, consts, False)
    return (y_prompt, y_sample)
```

```python
import functools

import jax
import jax.numpy as jnp
from jax import lax
from jax.experimental import pallas as pl
from jax.experimental.pallas import tpu as pltpu
from jax.experimental.pallas import tpu_sc as plsc

F32 = jnp.float32
BF16 = jnp.bfloat16
U32 = jnp.uint32

D_MODEL = 1024
HEAD_DIM = 64
D_LRU = 512
LRU_BLOCKS = 8
LRU_C = 8.0
CONV_W = 4
CONV_LEFT = 2
ATT_HEADS = 8
KV_HEADS = 2
Q_GROUP = ATT_HEADS // KV_HEADS
D_ATT = ATT_HEADS * HEAD_DIM
KV_DIM = KV_HEADS * HEAD_DIM
D_IN = 2 * D_LRU + D_ATT + 2 * KV_DIM
GRID_W = 64
ROPE_THETA = 10000.0
ROPE_AXIS_FREQS = HEAD_DIM // 4
N_MEM = 256
X_HEADS = 4
X_HEAD_DIM = D_MODEL // X_HEADS
N_EXPERTS = 64
TOP_K = 8
N_GROUPS = 8
GROUP_SIZE = N_EXPERTS // N_GROUPS
TOPK_GROUPS = 4
D_EXPERT = 256
PACKED = D_MODEL // 2
ROUTED_SCALE = 2.5
LN_EPS = 1e-5
RMS_EPS = 1e-6
LOG2_E = 1.4426950408889634

LANES = 128
SUBLANES = 8
MXU_DIM = 256
TOK_TILE = 512
IN_TILE = 1024
LRU_TILE = 512
ATT_TQ = 256
ATT_TK = 2048
ATT_MAX_INLINE_STEPS = 4
MOE_ROWS = 512
EXPERT_BLOCKS_PER_STEP = 2
COMBINE_PARTS = 2
MOE_PART_TOKENS = 32768
ROW_CHUNKS = PACKED // LANES
SC_CORES = 2
SC_SUBCORES = 16
SC_WINDOW = 128
VMEM_LIMIT = 48 * 1024 * 1024


def _cparams(*sem):
    return pltpu.CompilerParams(dimension_semantics=sem, vmem_limit_bytes=VMEM_LIMIT)


def _full(shape):
    return pl.BlockSpec(shape, lambda *_: (0,) * len(shape))


def _layer_norm(x, g, b):
    mu = jnp.mean(x, axis=-1, keepdims=True)
    xc = x - mu
    var = jnp.mean(xc * xc, axis=-1, keepdims=True)
    return xc * lax.rsqrt(var + LN_EPS) * g + b


def _row_rms(x, g):
    return x * lax.rsqrt(jnp.mean(x * x, axis=-1, keepdims=True) + RMS_EPS) * g


def _dot(a, b):
    return jnp.dot(a, b, preferred_element_type=F32)


def _dot_nt(a, b):
    return lax.dot_general(a, b, (((1,), (1,)), ((), ())), preferred_element_type=F32)


def _pack_halves(x):
    w = x.shape[1] // 2
    lo = lax.bitcast_convert_type(x[:, :w].astype(BF16).astype(F32), U32)
    hi = lax.bitcast_convert_type(x[:, w:].astype(BF16).astype(F32), U32)
    return (lo >> 16) | (hi & jnp.uint32(0xFFFF0000))


def _unpack_halves(w):
    lo = lax.bitcast_convert_type(w << 16, F32)
    hi = lax.bitcast_convert_type(w & jnp.uint32(0xFFFF0000), F32)
    return lo, hi


def _store_chunks(ref, w):
    for j in range(ROW_CHUNKS):
        ref[j] = w[:, j * LANES:(j + 1) * LANES]


def _load_chunks(ref):
    return jnp.concatenate([ref[j] for j in range(ROW_CHUNKS)], axis=1)


def _split_bf16(x):
    hi = x.astype(BF16)
    lo = (x - hi.astype(F32)).astype(BF16)
    return hi, lo


def _head_mean_square(z, avg_ref):
    hi, lo = _split_bf16(z * z)
    avg = avg_ref[...]
    w = avg.shape[0]
    parts = [_dot(hi[:, c:c + w], avg) + _dot(lo[:, c:c + w], avg) for c in range(0, z.shape[1], w)]
    return parts[0] if len(parts) == 1 else jnp.concatenate(parts, axis=1)


def _rope(x, cos, sin_signed):
    lane = lax.broadcasted_iota(jnp.int32, (1, LANES), 1)
    low_half = (lane % HEAD_DIM) < (HEAD_DIM // 2)
    outs = []
    for c in range(x.shape[1] // LANES):
        xc = x[:, c * LANES:(c + 1) * LANES]
        partner = jnp.where(low_half, pltpu.roll(xc, LANES - HEAD_DIM // 2, 1), pltpu.roll(xc, HEAD_DIM // 2, 1))
        outs.append(xc * cos + partner * sin_signed)
    return outs[0] if len(outs) == 1 else jnp.concatenate(outs, axis=1)


def _in_proj_kernel(x_ref, lg_ref, lb_ref, w_ref, gq_ref, gk_ref, cos_ref, sin_ref, avgq_ref, avgk_ref,
                    g_out, xbr_out, q_out, k_out, v_out):
    xn = _layer_norm(x_ref[...], lg_ref[...], lb_ref[...])
    z = _dot(xn.astype(BF16), w_ref[...])
    g_out[...] = z[:, :D_LRU]
    xbr_out[...] = z[:, D_LRU:2 * D_LRU]
    zq = z[:, 2 * D_LRU:2 * D_LRU + D_ATT]
    zk = z[:, 2 * D_LRU + D_ATT:2 * D_LRU + D_ATT + KV_DIM]
    v_out[...] = z[:, 2 * D_LRU + D_ATT + KV_DIM:].astype(BF16)
    cos = cos_ref[...]
    sin = sin_ref[...]
    q = zq * lax.rsqrt(_head_mean_square(zq, avgq_ref) + RMS_EPS) * gq_ref[...]
    k = zk * lax.rsqrt(_head_mean_square(zk, avgk_ref) + RMS_EPS) * gk_ref[...]
    q_out[...] = (_rope(q, cos, sin) * (HEAD_DIM ** -0.5 * LOG2_E)).astype(BF16)
    k_out[...] = _rope(k, cos, sin).astype(BF16)


def _in_proj(x, seq, ln_g, ln_b, w_in_b, gq, gk, cos_t, sin_t, avgq, avgk):
    n = x.shape[0]
    t = IN_TILE
    nseq = seq // t
    tok = lambda w: pl.BlockSpec((t, w), lambda i: (i, 0))
    pos = pl.BlockSpec((t, LANES), lambda i: (i % nseq, 0))
    return pl.pallas_call(
        _in_proj_kernel,
        grid=(n // t,),
        in_specs=[tok(D_MODEL), _full((1, D_MODEL)), _full((1, D_MODEL)), _full((D_MODEL, D_IN)),
                  _full((1, D_ATT)), _full((1, KV_DIM)), pos, pos, _full(avgq.shape), _full(avgk.shape)],
        out_specs=[tok(D_LRU), tok(D_LRU), tok(D_ATT), tok(KV_DIM), tok(KV_DIM)],
        out_shape=[jax.ShapeDtypeStruct((n, D_LRU), F32), jax.ShapeDtypeStruct((n, D_LRU), F32),
                   jax.ShapeDtypeStruct((n, D_ATT), BF16), jax.ShapeDtypeStruct((n, KV_DIM), BF16),
                   jax.ShapeDtypeStruct((n, KV_DIM), BF16)],
        compiler_params=_cparams("parallel"),
        name="in_proj",
    )(x, ln_g, ln_b, w_in_b, gq, gk, cos_t, sin_t, avgq, avgk)


def _softplus(x):
    return jnp.maximum(x, 0.0) + jnp.log1p(jnp.exp(-jnp.abs(x)))


def _lru_kernel(xp_ref, x_ref, xn_ref, cw_ref, cb_ref, w_ref, ba_ref, bx_ref, lam_ref, h_ref,
                a_s, b_s, c_s, *, tb, nt):
    d = pl.program_id(1)
    i = pl.program_id(2)
    tbi = jnp.where(d == 0, i, nt - 1 - i)
    x = x_ref[...]
    prev = jnp.where(tbi > 0, xp_ref[...], 0.0)
    nxt = jnp.where(tbi < nt - 1, xn_ref[...], 0.0)
    xe = jnp.concatenate([prev, x, nxt], axis=0)
    cw = cw_ref[...]
    o = SUBLANES - CONV_LEFT
    xc = xe[o:o + tb] * cw[0:1]
    for k in range(1, CONV_W):
        xc = xc + xe[o + k:o + k + tb] * cw[k:k + 1]
    xc = xc + cb_ref[...]
    rg = _dot(xc.astype(BF16), w_ref[...])
    r = jax.nn.sigmoid(rg[:, :D_LRU] + ba_ref[...])
    ig = jax.nn.sigmoid(rg[:, D_LRU:] + bx_ref[...])
    log_a = -LRU_C * r * _softplus(-lam_ref[...])
    a = jnp.exp(log_a)
    b = jnp.sqrt(-jnp.tanh(log_a) * (a * a + 1.0)) * (ig * xc)

    ngroups = tb // SUBLANES
    row = lax.broadcasted_iota(jnp.int32, (ngroups, SUBLANES, D_LRU), 1)

    @pl.when(i == 0)
    def _():
        c_s[...] = jnp.zeros_like(c_s)

    def scan(reverse):
        aa = a.reshape(ngroups, SUBLANES, D_LRU)
        bb = b.reshape(ngroups, SUBLANES, D_LRU)
        for k in (1, 2, 4):
            if reverse:
                valid = row < SUBLANES - k
                shift = SUBLANES - k
            else:
                valid = row >= k
                shift = k
            a_sh = jnp.where(valid, pltpu.roll(aa, shift, 1), 1.0)
            b_sh = jnp.where(valid, pltpu.roll(bb, shift, 1), 0.0)
            bb = aa * b_sh + bb
            aa = aa * a_sh
        a_s[...] = aa.reshape(tb, D_LRU)
        b_s[...] = bb.reshape(tb, D_LRU)

        def body(gi, c):
            g = (ngroups - 1 - gi) if reverse else gi
            sl = pl.ds(pl.multiple_of(g * SUBLANES, SUBLANES), SUBLANES)
            h = a_s[sl, :] * c + b_s[sl, :]
            h_ref[sl, :] = h
            edge = h[0:1, :] if reverse else h[SUBLANES - 1:SUBLANES, :]
            return jnp.broadcast_to(edge, (SUBLANES, D_LRU))

        c_s[...] = lax.fori_loop(0, ngroups, body, c_s[...], unroll=8)

    @pl.when(d == 0)
    def _():
        scan(False)

    @pl.when(d == 1)
    def _():
        scan(True)


def _lru(xbr, conv_w, conv_b, w_gates, ba, bx, lam):
    bsz, seq, _ = xbr.shape
    tb = LRU_TILE
    nt = seq // tb
    r8 = tb // SUBLANES
    nb8 = seq // SUBLANES

    def tblk(d, i):
        return jnp.where(d == 0, i, nt - 1 - i)

    cur = pl.BlockSpec((None, tb, D_LRU), lambda b, d, i: (b, tblk(d, i), 0))
    prv = pl.BlockSpec((None, SUBLANES, D_LRU), lambda b, d, i: (b, jnp.maximum(tblk(d, i) * r8 - 1, 0), 0))
    nxt = pl.BlockSpec((None, SUBLANES, D_LRU), lambda b, d, i: (b, jnp.minimum((tblk(d, i) + 1) * r8, nb8 - 1), 0))
    per_dir = lambda shape: pl.BlockSpec((None,) + shape, lambda b, d, i: (d,) + (0,) * len(shape))
    return pl.pallas_call(
        functools.partial(_lru_kernel, tb=tb, nt=nt),
        grid=(bsz, 2, nt),
        in_specs=[prv, cur, nxt, _full((CONV_W, D_LRU)), _full((1, D_LRU)),
                  per_dir((D_LRU, 2 * D_LRU)), per_dir((1, D_LRU)), per_dir((1, D_LRU)), per_dir((1, D_LRU))],
        out_specs=pl.BlockSpec((None, None, tb, D_LRU), lambda b, d, i: (d, b, tblk(d, i), 0)),
        out_shape=jax.ShapeDtypeStruct((2, bsz, seq, D_LRU), F32),
        scratch_shapes=[pltpu.VMEM((tb, D_LRU), F32), pltpu.VMEM((tb, D_LRU), F32),
                        pltpu.VMEM((SUBLANES, D_LRU), F32)],
        compiler_params=_cparams("parallel", "arbitrary", "arbitrary"),
        name="lru",
    )(xbr, xbr, xbr, conv_w, conv_b, w_gates, ba, bx, lam)


def _attn_kernel(q_ref, k_ref, v_ref, o_ref, qs_s, m_s, l_s, acc_s, *, tq, tk, seq):
    lane = lax.broadcasted_iota(jnp.int32, (1, LANES), 1)
    low = lane < HEAD_DIM
    nchunk = D_ATT // LANES
    for c in range(nchunk):
        qc = q_ref[:, c * LANES:(c + 1) * LANES]
        zero = jnp.zeros_like(qc)
        qs_s[c, :tq, :] = jnp.where(low, qc, zero)
        qs_s[c, tq:, :] = jnp.where(low, zero, qc)
    def step(j, first):
        sl = pl.ds(j * tk, tk) if isinstance(j, int) else pl.ds(pl.multiple_of(j * tk, tk), tk)
        kj = k_ref[sl, :]
        vj = v_ref[sl, :]
        for c in range(nchunk):
            s = _dot_nt(qs_s[c], kj)
            row_max = jnp.max(s, axis=1, keepdims=True)
            if first:
                m_next = jnp.broadcast_to(row_max, (2 * tq, LANES))
            else:
                m_prev = m_s[c]
                m_next = jnp.maximum(m_prev, row_max)
                alpha = jnp.exp2(m_prev - m_next)
            ps = [jnp.exp2(s[:, t * LANES:(t + 1) * LANES] - m_next) for t in range(tk // LANES)]
            part = ps[0]
            for pt in ps[1:]:
                part = part + pt
            pv = _dot(jnp.concatenate(ps, axis=1).astype(BF16), vj)
            if first:
                l_s[c] = part
                acc_s[c] = pv
            else:
                l_s[c] = alpha * l_s[c] + part
                acc_s[c] = alpha * acc_s[c] + pv
            m_s[c] = m_next

    step(0, True)
    nk = seq // tk
    if nk <= ATT_MAX_INLINE_STEPS:
        for j in range(1, nk):
            step(j, False)
    else:
        def body(j, carry):
            step(j, False)
            return carry

        lax.fori_loop(1, nk, body, 0)
    for c in range(nchunk):
        o = acc_s[c] / jnp.sum(l_s[c], axis=1, keepdims=True)
        o_ref[:, c * LANES:(c + 1) * LANES] = jnp.where(low, o[:tq], o[tq:])


def _attn(q, k, v):
    bsz, seq, _ = q.shape
    tq, tk = ATT_TQ, min(ATT_TK, seq)
    assert seq % tq == 0 and seq % tk == 0
    kv = pl.BlockSpec((None, seq, KV_DIM), lambda b, i: (b, 0, 0))
    qo = pl.BlockSpec((None, tq, D_ATT), lambda b, i: (b, i, 0))
    stat = pltpu.VMEM((D_ATT // LANES, 2 * tq, LANES), F32)
    return pl.pallas_call(
        functools.partial(_attn_kernel, tq=tq, tk=tk, seq=seq),
        grid=(bsz, seq // tq),
        in_specs=[qo, kv, kv],
        out_specs=qo,
        out_shape=jax.ShapeDtypeStruct((bsz, seq, D_ATT), F32),
        scratch_shapes=[pltpu.VMEM((D_ATT // LANES, 2 * tq, LANES), BF16), stat, stat, stat],
        compiler_params=_cparams("parallel", "parallel"),
        name="attn",
    )(q, k, v)


def _kv_mem_kernel(m_ref, w_ref, k_out, v_out):
    kv = _dot(m_ref[...].astype(BF16), w_ref[...])
    k_out[...] = kv[:, :D_MODEL].astype(BF16)
    v_out[...] = kv[:, D_MODEL:].astype(BF16)


def _kv_mem(mem, wkv_b):
    rows = mem.shape[0]
    blk = pl.BlockSpec((N_MEM, D_MODEL), lambda i: (i, 0))
    return pl.pallas_call(
        _kv_mem_kernel,
        grid=(rows // N_MEM,),
        in_specs=[blk, _full((D_MODEL, 2 * D_MODEL))],
        out_specs=[blk, blk],
        out_shape=[jax.ShapeDtypeStruct((rows, D_MODEL), BF16)] * 2,
        compiler_params=_cparams("parallel"),
        name="kv_mem",
    )(mem, wkv_b)


def _post_kernel(x_ref, g_ref, hf_ref, hb_ref, ya_ref, km_ref, vm_ref,
                 lg_ref, lb_ref, gl_ref, ga_ref, wo1_ref, wo2_ref, l1g_ref, l1b_ref,
                 wq_ref, wxo_ref, l2g_ref, l2b_ref, rwh_ref, rwl_ref, rb_ref, tri_ref, sg_ref, su_ref, sd_ref,
                 x2w_out, eidx_out, gate_out, rank_out, cnt_out, base_out, cnt_s, *, alpha):
    x0 = _layer_norm(x_ref[...], lg_ref[...], lb_ref[...])
    y_lru = (hf_ref[...] + hb_ref[...]) * jax.nn.gelu(g_ref[...], approximate=True)
    y_lru = _row_rms(y_lru, gl_ref[...])
    y_att = _row_rms(ya_ref[...], ga_ref[...])
    y = _dot(y_lru.astype(BF16), wo1_ref[...]) + _dot(y_att.astype(BF16), wo2_ref[...])
    x1 = _layer_norm(alpha * x0 + y, l1g_ref[...], l1b_ref[...])
    xq = _dot(x1.astype(BF16), wq_ref[...]).astype(BF16)
    heads = []
    for h in range(X_HEADS):
        sl = slice(h * X_HEAD_DIM, (h + 1) * X_HEAD_DIM)
        s = _dot_nt(xq[:, sl], km_ref[:, sl]) * (X_HEAD_DIM ** -0.5)
        e = jnp.exp(s - jnp.max(s, axis=1, keepdims=True))
        o = _dot(e.astype(BF16), vm_ref[:, sl]) / jnp.sum(e, axis=1, keepdims=True)
        heads.append(o.astype(BF16))
    o = jnp.concatenate(heads, axis=1)
    x2 = _layer_norm(alpha * x1 + _dot(o, wxo_ref[...]), l2g_ref[...], l2b_ref[...])
    _store_chunks(x2w_out, _pack_halves(x2))
    _route(x2, rwh_ref, rwl_ref, rb_ref, tri_ref, sg_ref, su_ref, sd_ref,
           eidx_out, gate_out, rank_out, cnt_out, base_out, cnt_s, alpha)


def _post(x, g, h, y_att, kmem, vmem, seq, alpha, ln_g, ln_b, gl, ga, wo1, wo2, l1g, l1b, wq, wxo, l2g, l2b,
          rwh, rwl, rb, tri, sg, su, sd, first, tiles):
    t = TOK_TILE
    n = tiles * t
    nseq = seq // t
    tok = lambda w: pl.BlockSpec((t, w), lambda i: (i + first, 0))
    out_tok = lambda w: pl.BlockSpec((t, w), lambda i: (i, 0))
    hdir = lambda d: pl.BlockSpec((None, t, D_LRU), lambda i: (d, i + first, 0))
    mem = pl.BlockSpec((N_MEM, D_MODEL), lambda i: ((i + first) // nseq, 0))
    vec = _full((1, D_MODEL))
    half = _full((1, D_LRU))
    sq = _full((D_MODEL, D_MODEL))
    kt = pl.BlockSpec((TOP_K, t), lambda i: (0, i))
    return pl.pallas_call(
        functools.partial(_post_kernel, alpha=alpha),
        grid=(tiles,),
        in_specs=[tok(D_MODEL), tok(D_LRU), hdir(0), hdir(1), tok(D_ATT), mem, mem,
                  vec, vec, half, half, _full((D_LRU, D_MODEL)), _full((D_ATT, D_MODEL)), vec, vec,
                  sq, sq, vec, vec,
                  _full((N_EXPERTS, D_MODEL)), _full((N_EXPERTS, D_MODEL)), _full((N_EXPERTS, 1)), _full((t, t)),
                  _full((D_MODEL, D_EXPERT)), _full((D_MODEL, D_EXPERT)), _full((D_EXPERT, D_MODEL))],
        out_specs=[pl.BlockSpec((ROW_CHUNKS, t, LANES), lambda i: (0, i, 0)), kt, kt, kt,
                   _full((N_EXPERTS, 1)), out_tok(D_MODEL)],
        out_shape=[jax.ShapeDtypeStruct((ROW_CHUNKS, n, LANES), U32),
                   jax.ShapeDtypeStruct((TOP_K, n), jnp.int32), jax.ShapeDtypeStruct((TOP_K, n), F32),
                   jax.ShapeDtypeStruct((TOP_K, n), jnp.int32), jax.ShapeDtypeStruct((N_EXPERTS, 1), jnp.int32),
                   jax.ShapeDtypeStruct((n, D_MODEL), F32)],
        scratch_shapes=[pltpu.VMEM((N_EXPERTS, 1), F32)],
        compiler_params=_cparams("arbitrary"),
        name="post",
    )(x, g, h, h, y_att, kmem, vmem, ln_g, ln_b, gl, ga, wo1, wo2, l1g, l1b, wq, wxo, l2g, l2b,
      rwh, rwl, rb, tri, sg, su, sd)


def _first_max(v, n):
    rows = lax.broadcasted_iota(jnp.int32, v.shape, 0)
    m = jnp.max(v, axis=0, keepdims=True)
    idx = jnp.min(jnp.where(v == m, rows, n), axis=0, keepdims=True)
    return m, idx, rows == idx


def _route(x, rwh_ref, rwl_ref, rb_ref, tri_ref, sg_ref, su_ref, sd_ref,
           eidx_out, gate_out, rank_out, cnt_out, base_out, cnt_s, alpha):
    @pl.when(pl.program_id(0) == 0)
    def _():
        cnt_s[...] = jnp.zeros_like(cnt_s)

    xh, xl = _split_bf16(x)
    logits = _dot_nt(rwh_ref[...], xh) + _dot_nt(rwh_ref[...], xl) + _dot_nt(rwl_ref[...], xh)
    scores = jax.nn.sigmoid(logits)
    biased = scores + rb_ref[...]
    t = x.shape[0]
    neg = -jnp.inf

    gscore = []
    for g in range(N_GROUPS):
        vg = biased[g * GROUP_SIZE:(g + 1) * GROUP_SIZE, :]
        m1, _, hit = _first_max(vg, GROUP_SIZE)
        m2 = jnp.max(jnp.where(hit, neg, vg), axis=0, keepdims=True)
        gscore.append(m1 + m2)
    gs = jnp.concatenate(gscore, axis=0)
    gsel = jnp.zeros(gs.shape, jnp.bool_)
    for _ in range(TOPK_GROUPS):
        _, _, hit = _first_max(gs, N_GROUPS)
        gsel = jnp.logical_or(gsel, hit)
        gs = jnp.where(hit, neg, gs)
    emask = jnp.concatenate(
        [jnp.broadcast_to(gsel[g:g + 1, :], (GROUP_SIZE, t)) for g in range(N_GROUPS)], axis=0)
    masked = jnp.where(emask, biased, neg)

    idxs, gates, hits = [], [], []
    sel = jnp.zeros(masked.shape, jnp.bool_)
    for _ in range(TOP_K):
        _, idx, hit = _first_max(masked, N_EXPERTS)
        idxs.append(idx)
        hits.append(hit)
        gates.append(jnp.sum(jnp.where(hit, scores, 0.0), axis=0, keepdims=True))
        sel = jnp.logical_or(sel, hit)
        masked = jnp.where(hit, neg, masked)
    gsum = gates[0]
    for gk in gates[1:]:
        gsum = gsum + gk
    gate = jnp.concatenate(gates, axis=0) / gsum * ROUTED_SCALE

    chosen = jnp.where(sel, 1.0, 0.0)
    before = _dot(chosen.astype(BF16), tri_ref[...]) + cnt_s[...]
    ranks = [jnp.sum(jnp.where(hit, before, 0.0), axis=0, keepdims=True) for hit in hits]
    cnt = cnt_s[...] + jnp.sum(chosen, axis=1, keepdims=True)
    cnt_s[...] = cnt

    eidx_out[...] = jnp.concatenate(idxs, axis=0)
    gate_out[...] = gate
    rank_out[...] = jnp.concatenate(ranks, axis=0).astype(jnp.int32)
    cnt_out[...] = cnt.astype(jnp.int32)

    hmid = jax.nn.silu(_dot(xh, sg_ref[...])) * _dot(xh, su_ref[...])
    base_out[...] = alpha * x + _dot(hmid.astype(BF16), sd_ref[...])


def _sc_mesh():
    return plsc.VectorSubcoreMesh(core_axis_name="core", subcore_axis_name="subcore")


def _sc_worker_base(per_worker):
    return (lax.axis_index("subcore") * SC_CORES + lax.axis_index("core")) * per_worker


def _sc_dispatch(x_sub, sidx, rows_out):
    nsub = x_sub.shape[0]
    win = SC_WINDOW
    per_worker = nsub // (SC_CORES * SC_SUBCORES)
    assert per_worker * SC_CORES * SC_SUBCORES == nsub and per_worker % win == 0

    nwin = per_worker // win
    assert nwin % 2 == 0

    def body(x_hbm, idx_hbm, out_hbm, idx_v, rows_v, lsem0, lsem1, ssem0, ssem1):
        base = _sc_worker_base(per_worker)
        lsem = (lsem0, lsem1)
        ssem = (ssem0, ssem1)

        def loads(ci, b):
            off = base + ci * win
            return (pltpu.make_async_copy(idx_hbm.at[:, pl.ds(off, win)], idx_v.at[b], lsem[b]),
                    pltpu.make_async_copy(x_hbm.at[pl.ds(off, win)], rows_v.at[b], lsem[b]))

        def scatters(b):
            return [pltpu.make_async_copy(rows_v.at[b], out_hbm.at[idx_v.at[b, k]], ssem[b]) for k in range(TOP_K)]

        for cp in loads(0, 0):
            cp.start()

        @pl.loop(0, nwin, step=2)
        def _(c0):
            for b in range(2):
                ci = c0 + b
                for cp in loads(ci, b):
                    cp.wait()
                for cp in scatters(b):
                    cp.start()

                @pl.when(ci >= 1)
                def _():
                    for cp in scatters(1 - b):
                        cp.wait()

                @pl.when(ci + 1 < nwin)
                def _():
                    for cp in loads(ci + 1, 1 - b):
                        cp.start()

        for cp in scatters(1):
            cp.wait()

    return pl.kernel(
        body, mesh=_sc_mesh(), out_type=jax.ShapeDtypeStruct((rows_out, LANES), U32),
        scratch_types=[pltpu.VMEM((2, TOP_K, win), jnp.int32), pltpu.VMEM((2, win, LANES), U32)]
        + [pltpu.SemaphoreType.DMA] * 4,
    )(x_sub, sidx)


def _sc_gather(y_sub, sidx):
    nsub = sidx.shape[1]
    win = SC_WINDOW
    pair = 2
    npairs = TOP_K // pair
    per_worker = nsub // (SC_CORES * SC_SUBCORES)
    assert per_worker * SC_CORES * SC_SUBCORES == nsub and per_worker % win == 0

    def body(y_hbm, idx_hbm, out_hbm, idx_v, rows_v, gsem, osem0, osem1):
        base = _sc_worker_base(per_worker)
        osem = (osem0, osem1)

        def gathers(q, b):
            return [pltpu.make_async_copy(y_hbm.at[idx_v.at[q * pair + j]], rows_v.at[b, j], gsem) for j in range(pair)]

        def stores(off, q, b):
            return [pltpu.make_async_copy(rows_v.at[b, j], out_hbm.at[q * pair + j, pl.ds(off, win)], osem[b])
                    for j in range(pair)]

        @pl.loop(0, per_worker // win)
        def _(ci):
            off = base + ci * win
            pltpu.sync_copy(idx_hbm.at[:, pl.ds(off, win)], idx_v)
            for q in range(npairs):
                b = q % 2
                if q >= 2:
                    for cp in stores(off, q - 2, b):
                        cp.wait()
                copies = gathers(q, b)
                for cp in copies:
                    cp.start()
                for cp in copies:
                    cp.wait()
                for cp in stores(off, q, b):
                    cp.start()
            for q in (npairs - 2, npairs - 1):
                for cp in stores(off, q, q % 2):
                    cp.wait()

    return pl.kernel(
        body, mesh=_sc_mesh(), out_type=jax.ShapeDtypeStruct((TOP_K, nsub, LANES), U32),
        scratch_types=[pltpu.VMEM((TOP_K, win), jnp.int32), pltpu.VMEM((2, pair, win, LANES), U32)]
        + [pltpu.SemaphoreType.DMA] * 3,
    )(y_sub, sidx)


def _experts_kernel(be_ref, nv_ref, xs_ref, *refs):
    weights, y_out = refs[:-1], refs[-1]
    r = MOE_ROWS

    @pl.when(pl.program_id(0) * EXPERT_BLOCKS_PER_STEP < nv_ref[0])
    def _():
        for sub in range(EXPERT_BLOCKS_PER_STEP):
            wg_ref, wu_ref, wd_ref = weights[3 * sub:3 * sub + 3]
            rows = slice(sub * r, (sub + 1) * r)
            lo, hi = _unpack_halves(jnp.concatenate([xs_ref[j, rows, :] for j in range(ROW_CHUNKS)], axis=1))
            lo = lo.astype(BF16)
            hi = hi.astype(BF16)
            gate = _dot(lo, wg_ref[:PACKED, :]) + _dot(hi, wg_ref[PACKED:, :])
            up = _dot(lo, wu_ref[:PACKED, :]) + _dot(hi, wu_ref[PACKED:, :])
            hmid = jax.nn.silu(gate) * up
            y = _pack_halves(_dot(hmid.astype(BF16), wd_ref[...]))
            for j in range(ROW_CHUNKS):
                y_out[j, rows, :] = y[:, j * LANES:(j + 1) * LANES]

    @pl.when(pl.program_id(0) * EXPERT_BLOCKS_PER_STEP >= nv_ref[0])
    def _():
        y_out[...] = jnp.zeros_like(y_out)


def _experts(block_e, nvalid, xs, wg, wu, wd):
    rows = xs.shape[1]
    nsub = EXPERT_BLOCKS_PER_STEP
    step_rows = MOE_ROWS * nsub
    assert rows % step_rows == 0
    blk = pl.BlockSpec((ROW_CHUNKS, step_rows, LANES), lambda i, be, nv: (0, i, 0))
    w_specs = []
    for sub in range(nsub):
        pick = lambda i, be, nv, sub=sub: (be[i * nsub + sub], 0, 0)
        w_specs += [pl.BlockSpec((None, D_MODEL, D_EXPERT), pick), pl.BlockSpec((None, D_MODEL, D_EXPERT), pick),
                    pl.BlockSpec((None, D_EXPERT, D_MODEL), pick)]
    return pl.pallas_call(
        _experts_kernel,
        grid_spec=pltpu.PrefetchScalarGridSpec(
            num_scalar_prefetch=2,
            grid=(rows // step_rows,),
            in_specs=[blk] + w_specs,
            out_specs=blk,
        ),
        out_shape=jax.ShapeDtypeStruct((ROW_CHUNKS, rows, LANES), U32),
        compiler_params=_cparams("arbitrary"),
        name="experts",
    )(block_e, nvalid, xs, *([wg, wu, wd] * nsub))


def _combine_kernel(base_ref, yg_ref, gate_ref, lg_ref, lb_ref, *refs):
    o_ref = refs[-1]
    gate = gate_ref[...]
    routed = None
    for k in range(TOP_K):
        lo, hi = _unpack_halves(_load_chunks(yg_ref.at[k]))
        yk = jnp.concatenate([lo, hi], axis=1) * gate[:, k:k + 1]
        routed = yk if routed is None else routed + yk
    o_ref[...] = _layer_norm(base_ref[...] + routed, lg_ref[...], lb_ref[...])


def _combine(base, yg, gate_t, lg, lb, prev, first, out_first, n):
    t = TOK_TILE
    steps = yg.shape[2] // t
    tok = pl.BlockSpec((t, D_MODEL), lambda i: (i + first, 0))
    in_specs = [tok, pl.BlockSpec((TOP_K, ROW_CHUNKS, t, LANES), lambda i: (0, 0, i, 0)),
                pl.BlockSpec((t, TOP_K), lambda i: (i + first, 0)), _full((1, D_MODEL)), _full((1, D_MODEL))]
    args = [base, yg, gate_t, lg, lb]
    aliases = {}
    if prev is not None:
        in_specs.append(pl.BlockSpec(memory_space=pl.ANY))
        args.append(prev)
        aliases = {len(args) - 1: 0}
    return pl.pallas_call(
        _combine_kernel,
        grid=(steps,),
        in_specs=in_specs,
        out_specs=pl.BlockSpec((t, D_MODEL), lambda i: (i + out_first, 0)),
        out_shape=jax.ShapeDtypeStruct((n, D_MODEL), F32),
        input_output_aliases=aliases,
        compiler_params=_cparams("parallel"),
        name="combine",
    )(*args)


def _q_lane_perm():
    order = []
    for c in range(Q_GROUP):
        for g in range(KV_HEADS):
            order.append(g * Q_GROUP + c)
    idx = []
    for h in order:
        idx.extend(range(h * HEAD_DIM, (h + 1) * HEAD_DIM))
    return jnp.asarray(idx, jnp.int32)


def _block_diag(w):
    nb, bi, bo = w.shape
    eye = jnp.eye(nb, dtype=w.dtype)
    return (eye[:, None, :, None] * w[:, :, None, :]).reshape(nb * bi, nb * bo)


def _head_avg(width):
    h = jnp.arange(width) // HEAD_DIM
    return jnp.where(h[:, None] == h[None, :], 1.0 / HEAD_DIM, 0.0).astype(BF16)


def _rope_tables(seq):
    rows = seq // GRID_W
    row_id, col_id = jnp.meshgrid(jnp.arange(rows), jnp.arange(GRID_W), indexing='ij')
    row_id = row_id.reshape(-1).astype(F32)
    col_id = col_id.reshape(-1).astype(F32)
    inv_freq = ROPE_THETA ** (-jnp.arange(ROPE_AXIS_FREQS, dtype=F32) / ROPE_AXIS_FREQS)
    ang = jnp.concatenate([row_id[:, None] * inv_freq, col_id[:, None] * inv_freq], axis=-1)
    cos, sin = jnp.cos(ang), jnp.sin(ang)
    reps = LANES // HEAD_DIM
    return jnp.tile(jnp.concatenate([cos, cos], axis=1), (1, reps)), jnp.tile(jnp.concatenate([-sin, sin], axis=1), (1, reps))


def _prep_layer(p):
    perm = _q_lane_perm()
    w_in = p['w_in']
    q0 = 2 * D_LRU
    w_in = jnp.concatenate([w_in[:, :q0], w_in[:, q0:q0 + D_ATT][:, perm], w_in[:, q0 + D_ATT:]], axis=1)
    w_out = p['w_out']
    row = lambda v: v.reshape(1, -1)
    rw_t = p['router_w'].T
    rwh = rw_t.astype(BF16)
    return dict(
        w_in=w_in.astype(BF16),
        gq=jnp.tile(p['q_norm_g'], ATT_HEADS).reshape(1, D_ATT),
        gk=jnp.tile(p['k_norm_g'], KV_HEADS).reshape(1, KV_DIM),
        conv_w=p['conv_w'], conv_b=row(p['conv_b']),
        w_gates=jnp.stack([jnp.concatenate([_block_diag(p['lru_wa'][d]), _block_diag(p['lru_wx'][d])], axis=1)
                           for d in range(2)]).astype(BF16),
        ba=p['lru_ba'].reshape(2, 1, D_LRU), bx=p['lru_bx'].reshape(2, 1, D_LRU),
        lam=p['lru_lambda'].reshape(2, 1, D_LRU),
        gl=row(p['gn_lru_g']), ga=row(p['gn_att_g'][perm]),
        wo1=w_out[:D_LRU].astype(BF16), wo2=w_out[D_LRU:][perm].astype(BF16),
        l1g=row(p['ln1_g']), l1b=row(p['ln1_b']),
        wq=p['xa_wq'].astype(BF16), wkv=p['xa_wkv'].astype(BF16), wxo=p['xa_wo'].astype(BF16),
        l2g=row(p['ln2_g']), l2b=row(p['ln2_b']),
        rwh=rwh, rwl=(rw_t - rwh.astype(F32)).astype(BF16), rb=p['router_b'].reshape(N_EXPERTS, 1),
        wg=p['w_gate'].astype(BF16), wu=p['w_up'].astype(BF16), wd=p['w_down'].astype(BF16),
        sg=p['sh_gate'].astype(BF16), su=p['sh_up'].astype(BF16), sd=p['sh_down'].astype(BF16),
        l3g=row(p['ln3_g']), l3b=row(p['ln3_b']),
    )


def _moe(x2w, eidx, gate, rank, counts, lp):
    n = eidx.shape[1]
    r = MOE_ROWS
    n_blocks = (n * TOP_K) // r + N_EXPERTS
    rows = n_blocks * r
    counts = counts.reshape(N_EXPERTS)
    pad_counts = ((counts + r - 1) // r) * r
    pad_end = jnp.cumsum(pad_counts)
    pad_start = pad_end - pad_counts
    start_of = jnp.sum(jnp.where(eidx[:, :, None] == jnp.arange(N_EXPERTS), pad_start, 0), axis=-1)
    dest = start_of + rank
    block_e = jnp.minimum(jnp.sum(pad_end[None, :] // r <= jnp.arange(n_blocks)[:, None], axis=1),
                          N_EXPERTS - 1).astype(jnp.int32)
    nvalid = (pad_end[-1] // r).astype(jnp.int32).reshape(1)
    def sub_rows(d):
        return (d[:, None, :] + (jnp.arange(ROW_CHUNKS, dtype=jnp.int32) * rows)[None, :, None]).reshape(TOP_K, -1)

    xs = _sc_dispatch(x2w.reshape(ROW_CHUNKS * n, LANES), sub_rows(dest), ROW_CHUNKS * rows)
    ys = _experts(block_e, nvalid, xs.reshape(ROW_CHUNKS, rows, LANES), lp['wg'], lp['wu'], lp['wd'])
    ys = ys.reshape(ROW_CHUNKS * rows, LANES)
    part = n // COMBINE_PARTS
    ygs = [_sc_gather(ys, sub_rows(dest[:, p * part:(p + 1) * part])).reshape(TOP_K, ROW_CHUNKS, part, LANES)
           for p in range(COMBINE_PARTS)]
    return ygs, gate.T


def _layer(x, mem, lp, alpha, consts, first):
    bsz, seq, _ = x.shape
    n = bsz * seq
    xf = x.reshape(n, D_MODEL)
    cos_t, sin_t = _rope_tables(seq)
    g, xbr, q, k, v = _in_proj(xf, seq, consts['ln_g'], consts['ln_b'], lp['w_in'], lp['gq'], lp['gk'],
                               cos_t, sin_t, consts['avgq'], consts['avgk'])
    h = _lru(xbr.reshape(bsz, seq, D_LRU), lp['conv_w'], lp['conv_b'], lp['w_gates'], lp['ba'], lp['bx'], lp['lam'])
    y_att = _attn(q.reshape(bsz, seq, D_ATT), k.reshape(bsz, seq, KV_DIM), v.reshape(bsz, seq, KV_DIM))
    kmem, vmem = _kv_mem(mem.reshape(bsz * N_MEM, D_MODEL), lp['wkv'])
    tiles = n // TOK_TILE
    nparts = max(1, n // MOE_PART_TOKENS)
    part_tiles = tiles // nparts
    out = None
    for p in range(nparts):
        x2w, eidx, gate, rank, counts, base = _post(
            xf, g, h.reshape(2, n, D_LRU), y_att.reshape(n, D_ATT), kmem, vmem, seq, alpha,
            consts['ln_g'], consts['ln_b'], lp['gl'], lp['ga'], lp['wo1'], lp['wo2'], lp['l1g'], lp['l1b'],
            lp['wq'], lp['wxo'], lp['l2g'], lp['l2b'],
            lp['rwh'], lp['rwl'], lp['rb'], consts['tri'], lp['sg'], lp['su'], lp['sd'],
            p * part_tiles, part_tiles)
        ygs, gate_t = _moe(x2w, eidx, gate, rank, counts, lp)
        for q, yg in enumerate(ygs):
            steps = yg.shape[2] // TOK_TILE
            out = _combine(base, yg, gate_t, lp['l3g'], lp['l3b'], out, q * steps, p * part_tiles + q * steps, n)
    return out.reshape(bsz, seq, D_MODEL)


def kernel(x_prompt, x_sample, mem_prompt, mem_sample, ln_in_g, ln_in_b, w_in, conv_w, conv_b, lru_wa, lru_ba, lru_wx, lru_bx, lru_lambda, q_norm_g, k_norm_g, gn_lru_g, gn_att_g, w_out, ln1_g, ln1_b, xa_wq, xa_wkv, xa_wo, ln2_g, ln2_b, router_w, router_b, w_gate, w_up, w_down, sh_gate, sh_up, sh_down, ln3_g, ln3_b):
    depth = w_in.shape[0]
    assert depth == 1, "the fused in_proj/post kernels assume the input LayerNorm feeds a single layer"
    alpha = (2.0 * depth) ** 0.25
    stacked = dict(w_in=w_in, conv_w=conv_w, conv_b=conv_b, lru_wa=lru_wa, lru_ba=lru_ba, lru_wx=lru_wx,
                   lru_bx=lru_bx, lru_lambda=lru_lambda, q_norm_g=q_norm_g, k_norm_g=k_norm_g, gn_lru_g=gn_lru_g,
                   gn_att_g=gn_att_g, w_out=w_out, ln1_g=ln1_g, ln1_b=ln1_b, xa_wq=xa_wq, xa_wkv=xa_wkv,
                   xa_wo=xa_wo, ln2_g=ln2_g, ln2_b=ln2_b, router_w=router_w, router_b=router_b, w_gate=w_gate,
                   w_up=w_up, w_down=w_down, sh_gate=sh_gate, sh_up=sh_up, sh_down=sh_down, ln3_g=ln3_g, ln3_b=ln3_b)
    lp = _prep_layer({name: val[0] for name, val in stacked.items()})
    tri = (jnp.arange(TOK_TILE)[:, None] < jnp.arange(TOK_TILE)[None, :]).astype(BF16)
    consts = dict(ln_g=ln_in_g.reshape(1, D_MODEL), ln_b=ln_in_b.reshape(1, D_MODEL),
                  avgq=_head_avg(min(D_ATT, MXU_DIM)), avgk=_head_avg(min(KV_DIM, MXU_DIM)), tri=tri)
    y_prompt = _layer(x_prompt, mem_prompt, lp, alpha, consts, True)
    y_sample = _layer(x_sample, mem_sample, lp, alpha, consts, False)
    return (y_prompt, y_sample)
```
